```python
import math
import jax, jax.numpy as jnp
from jax import lax
import numpy as np

D_MODEL = 2048
BATCH = 16
SEQ = 2048
DEPTH = 4

N_MIXERS = 2
N_ATTN_LAYERS = (DEPTH + 1) // 2
N_SSM_LAYERS = DEPTH // 2

N_HEADS = 16
HEAD_DIM = D_MODEL // N_HEADS
DILATED_GROUPS = ((128, 1), (512, 4), (2048, 16))
N_DGROUPS = len(DILATED_GROUPS)
ATTN_BLOCK = 128
QK_COLS = N_DGROUPS * N_HEADS * HEAD_DIM
V_COLS = N_HEADS * HEAD_DIM
ATTN_IN_COLS = 2 * QK_COLS + V_COLS

SSM_GROUP = 16
SSM_NGROUPS = D_MODEL // SSM_GROUP
SSM_STATE = 64
STEP_MIN = 0.001
STEP_MAX = 0.1

D_FF = 5632
CONV_W = 3

EPS = 1e-6

kernel_name = "hybrid_dilated_attn_s5_convffn_adaln"


def rms_norm(x, g):
    xf = x.astype(jnp.float32)
    y = xf * lax.rsqrt(jnp.mean(xf * xf, axis=-1, keepdims=True) + EPS)
    return (y * g.astype(jnp.float32)).astype(x.dtype)


def alibi_slopes():
    return 2.0 ** (-8.0 * jnp.arange(1, N_HEADS + 1, dtype=jnp.float32) / N_HEADS)


def dilated_branch(q, k, v, window, dilation):
    B, T, H, dh = q.shape
    blk = ATTN_BLOCK
    span = window // dilation
    L = -(-T // dilation)
    nb = -(-L // blk)
    Lp = nb * blk
    Tp = Lp * dilation

    def to_classes(a):
        a = jnp.pad(a, ((0, 0), (0, Tp - T), (0, 0), (0, 0)))
        a = a.reshape(B, Lp, dilation, H, dh).transpose(0, 2, 1, 3, 4)
        return a.reshape(B, dilation, nb, blk, H, dh)

    def with_prev(a):
        prev = jnp.pad(a, ((0, 0), (0, 0), (1, 0), (0, 0), (0, 0), (0, 0)))[:, :, :-1]
        return jnp.concatenate([prev, a], axis=3)

    qb = to_classes(q)
    kk = with_prev(to_classes(k))
    vv = with_prev(to_classes(v))

    s = jnp.einsum('brnqhc,brnkhc->brnhqk', qb, kk,
                   preferred_element_type=jnp.float32) * (HEAD_DIM ** -0.5)
    qi = jnp.arange(blk)[:, None]
    kj = jnp.arange(2 * blk)[None, :]
    dist = blk + qi - kj
    nidx = jnp.arange(nb)[:, None, None]
    valid = (dist >= 0) & (dist <= span) & (nidx * blk + kj - blk >= 0)
    bias = -alibi_slopes()[:, None, None] * (dist * dilation).astype(jnp.float32)
    s = jnp.where(valid[:, None], s + bias, -jnp.inf)
    m = jnp.max(s, axis=-1, keepdims=True)
    e = jnp.exp(s - m)
    den = jnp.sum(e, axis=-1, keepdims=True)
    lse = (m + jnp.log(den))[..., 0]
    p = e / den
    o = jnp.einsum('brnhqk,brnkhc->brnqhc', p, vv.astype(jnp.float32))
    o = o.reshape(B, dilation, Lp, H, dh).transpose(0, 2, 1, 3, 4).reshape(B, Tp, H, dh)[:, :T]
    lse = lse.transpose(0, 1, 2, 4, 3).reshape(B, dilation, Lp, H).transpose(0, 2, 1, 3)
    lse = lse.reshape(B, Tp, H)[:, :T]
    return o, lse


def dilated_attention(h, w_in, w_out):
    B, T, _ = h.shape
    proj = h @ w_in
    q, k, v = jnp.split(proj, [QK_COLS, 2 * QK_COLS], axis=-1)
    q = q.reshape(B, T, N_DGROUPS, N_HEADS, HEAD_DIM)
    k = k.reshape(B, T, N_DGROUPS, N_HEADS, HEAD_DIM)
    v = v.reshape(B, T, N_HEADS, HEAD_DIM)
    outs, lses = [], []
    for g, (window, dilation) in enumerate(DILATED_GROUPS):
        o, l = dilated_branch(q[:, :, g], k[:, :, g], v, window, dilation)
        outs.append(o)
        lses.append(l)
    wts = jax.nn.softmax(jnp.stack(lses, axis=0), axis=0)
    o = sum(wts[g][..., None] * outs[g] for g in range(N_DGROUPS))
    return o.reshape(B, T, V_COLS).astype(h.dtype) @ w_out


def _ssm_combine(left, right):
    a1r, a1i, b1r, b1i = left
    a2r, a2i, b2r, b2i = right
    return (a2r * a1r - a2i * a1i,
            a2r * a1i + a2i * a1r,
            a2r * b1r - a2i * b1i + b2r,
            a2r * b1i + a2i * b1r + b2i)


def _s5_core(ug, lb_re, lb_im, bb_re, bb_im, c_re, c_im):
    T = ug.shape[0]
    bu_re = jnp.einsum('tgc,gpc->tgp', ug, bb_re)
    bu_im = jnp.einsum('tgc,gpc->tgp', ug, bb_im)
    ar = jnp.broadcast_to(lb_re, (T,) + lb_re.shape)
    ai = jnp.broadcast_to(lb_im, (T,) + lb_im.shape)
    _, _, xr, xi = lax.associative_scan(_ssm_combine, (ar, ai, bu_re, bu_im), axis=0)
    return jnp.einsum('tgp,gcp->tgc', xr, c_re) - jnp.einsum('tgp,gcp->tgc', xi, c_im)


def s5_mixer(h, w_in, log_step, a_re, a_im, b_re, b_im, c_re, c_im, d_skip, w_gate, w_out):
    B, T, _ = h.shape
    f32 = jnp.float32
    u = (h @ w_in).astype(f32)
    ug = u.reshape(B, T, SSM_NGROUPS, SSM_GROUP)
    step = jnp.exp(log_step.astype(f32))[:, None]
    lr, li = a_re.astype(f32), a_im.astype(f32)
    mag = jnp.exp(lr * step)
    ang = li * step
    lb_re = mag * jnp.cos(ang)
    lb_im = mag * jnp.sin(ang)
    nr, ni = lb_re - 1.0, lb_im
    den = lr * lr + li * li
    coef_re = ((nr * lr + ni * li) / den)[..., None]
    coef_im = ((ni * lr - nr * li) / den)[..., None]
    br, bi = b_re.astype(f32), b_im.astype(f32)
    bb_re = coef_re * br - coef_im * bi
    bb_im = coef_re * bi + coef_im * br
    cr, ci = c_re.astype(f32), c_im.astype(f32)
    y = lax.map(lambda us: _s5_core(us, lb_re, lb_im, bb_re, bb_im, cr, ci), ug)
    y = y.reshape(B, T, D_MODEL) + d_skip.astype(f32) * u
    g = jax.nn.gelu(y).astype(h.dtype)
    g = g * jax.nn.sigmoid(g @ w_gate)
    return g @ w_out


def conv_ffn(h, w_up, conv_w, conv_b, w_down):
    T = h.shape[1]
    a = h @ w_up
    ap = jnp.pad(a, ((0, 0), (CONV_W - 1, 0), (0, 0)))
    a = conv_b + sum(conv_w[j] * ap[:, j:j + T] for j in range(CONV_W))
    gate, up = jnp.split(a, 2, axis=-1)
    return (jax.nn.silu(gate) * up) @ w_down


def setup_inputs(seed: int = 0) -> dict:
    key = jax.random.key(seed)
    ks = iter(jax.random.split(key, 32))
    f32 = jnp.float32

    def nrm(shape, std):
        return std * jax.random.normal(next(ks), shape, f32)

    D, F, G, P, C = D_MODEL, D_FF, SSM_NGROUPS, SSM_STATE, SSM_GROUP
    NA, NB = N_ATTN_LAYERS, N_SSM_LAYERS
    x = nrm((BATCH, SEQ, D), 1.0)
    c = nrm((BATCH, D), 1.0)
    ada_w = nrm((DEPTH, D, 6 * D), 0.3 * D ** -0.5)
    ada_b = nrm((DEPTH, 6 * D), 0.01)
    norm1_g = 1.0 + nrm((DEPTH, D), 0.05)
    norm2_g = 1.0 + nrm((DEPTH, D), 0.05)
    attn_w_in = nrm((NA, D, ATTN_IN_COLS), D ** -0.5)
    attn_w_out = nrm((NA, V_COLS, D), V_COLS ** -0.5)
    ssm_w_in = nrm((NB, D, D), D ** -0.5)
    ssm_log_step = jax.random.uniform(next(ks), (NB, G), f32,
                                      minval=math.log(STEP_MIN), maxval=math.log(STEP_MAX))
    ssm_a_re = -0.5 * jnp.exp(nrm((NB, G, P), 0.05))
    ssm_a_im = jnp.pi * jnp.arange(P, dtype=f32) + nrm((NB, G, P), 0.01)
    ssm_b_re = nrm((NB, G, P, C), (2.0 * C) ** -0.5)
    ssm_b_im = nrm((NB, G, P, C), (2.0 * C) ** -0.5)
    ssm_c_re = nrm((NB, G, C, P), P ** -0.5)
    ssm_c_im = nrm((NB, G, C, P), P ** -0.5)
    ssm_d = nrm((NB, D), 0.5)
    ssm_w_gate = nrm((NB, D, D), D ** -0.5)
    ssm_w_out = nrm((NB, D, D), D ** -0.5)
    ffn_w_up = nrm((DEPTH, D, 2 * F), D ** -0.5)
    ffn_conv_w = nrm((DEPTH, CONV_W, 2 * F), CONV_W ** -0.5)
    ffn_conv_b = nrm((DEPTH, 2 * F), 0.01)
    ffn_w_down = nrm((DEPTH, F, D), F ** -0.5)
    final_norm_g = 1.0 + nrm((D,), 0.05)
    return {"x": x, "c": c, "ada_w": ada_w, "ada_b": ada_b,
            "norm1_g": norm1_g, "norm2_g": norm2_g,
            "attn_w_in": attn_w_in, "attn_w_out": attn_w_out,
            "ssm_w_in": ssm_w_in, "ssm_log_step": ssm_log_step,
            "ssm_a_re": ssm_a_re, "ssm_a_im": ssm_a_im,
            "ssm_b_re": ssm_b_re, "ssm_b_im": ssm_b_im,
            "ssm_c_re": ssm_c_re, "ssm_c_im": ssm_c_im,
            "ssm_d": ssm_d, "ssm_w_gate": ssm_w_gate, "ssm_w_out": ssm_w_out,
            "ffn_w_up": ffn_w_up, "ffn_conv_w": ffn_conv_w, "ffn_conv_b": ffn_conv_b,
            "ffn_w_down": ffn_w_down, "final_norm_g": final_norm_g}


def reference(x, c, ada_w, ada_b, norm1_g, norm2_g, attn_w_in, attn_w_out,
              ssm_w_in, ssm_log_step, ssm_a_re, ssm_a_im, ssm_b_re, ssm_b_im,
              ssm_c_re, ssm_c_im, ssm_d, ssm_w_gate, ssm_w_out,
              ffn_w_up, ffn_conv_w, ffn_conv_b, ffn_w_down, final_norm_g):
    cs = jax.nn.silu(c)
    for i in range(DEPTH):
        j = i // N_MIXERS
        mod = (cs @ ada_w[i] + ada_b[i])[:, None, :]
        sh1, sc1, g1, sh2, sc2, g2 = jnp.split(mod, 6, axis=-1)
        h = rms_norm(x, norm1_g[i]) * (1.0 + sc1) + sh1
        if i % N_MIXERS == 0:
            mix = dilated_attention(h, attn_w_in[j], attn_w_out[j])
        else:
            mix = s5_mixer(h, ssm_w_in[j], ssm_log_step[j], ssm_a_re[j], ssm_a_im[j],
                           ssm_b_re[j], ssm_b_im[j], ssm_c_re[j], ssm_c_im[j],
                           ssm_d[j], ssm_w_gate[j], ssm_w_out[j])
        x = x + g1 * mix
        h = rms_norm(x, norm2_g[i]) * (1.0 + sc2) + sh2
        x = x + g2 * conv_ffn(h, ffn_w_up[i], ffn_conv_w[i], ffn_conv_b[i], ffn_w_down[i])
    return rms_norm(x, final_norm_g)
```

```python
import functools
import math

import jax
import jax.numpy as jnp
from jax import lax
from jax.experimental import pallas as pl
from jax.experimental.pallas import tpu as pltpu

F32 = jnp.float32
BF16 = jnp.bfloat16

D_MODEL = 2048
N_HEADS = 16
HEAD_DIM = 128
DILATED_GROUPS = ((128, 1), (512, 4), (2048, 16))
ATTN_BLOCK = 128
SSM_GROUP = 16
SSM_STATE = 64
D_FF = 5632
CONV_W = 3
EPS = 1e-6

V7X_LANES = 128
V7X_BF16_SUBLANES = 16
V7X_SCOPED_VMEM_CAP_BYTES = 60000 * 1024
MIB = 1024 * 1024

SSM_CHUNK = 256
SSM_CHUNK_GROUPS = SSM_CHUNK // SSM_GROUP
SSM_CHUNK_STATES = SSM_CHUNK_GROUPS * SSM_STATE
SSM_SCAN_LANES = 512
CONV_HALO = 16


def _params(semantics, vmem_bytes):
    return pltpu.CompilerParams(
        dimension_semantics=semantics,
        vmem_limit_bytes=int(min(vmem_bytes, V7X_SCOPED_VMEM_CAP_BYTES)))


def _sigmoid(x):
    return 1.0 / (1.0 + jnp.exp(-x))


def _norm_mod(x, g, sc, sh):
    ms = jnp.mean(x * x, axis=-1, keepdims=True)
    y = x * lax.rsqrt(ms + EPS)
    return (y * g) * (1.0 + sc) + sh


def _ada_kernel(c_ref, w_ref, b_ref, o_ref):
    c = c_ref[...]
    cs = (c * _sigmoid(c)).astype(BF16)
    o_ref[0] = jnp.dot(cs, w_ref[0].astype(BF16), preferred_element_type=F32) + b_ref[0]


def _ada(c, ada_w, ada_b, bn=1024):
    depth, d, n = ada_w.shape
    b = c.shape[0]
    vmem = 2 * (d * bn * 4) + d * bn * 2 + 4 * b * (d + bn) * 4 + 4 * MIB
    return pl.pallas_call(
        _ada_kernel,
        grid=(depth, n // bn),
        in_specs=[pl.BlockSpec((b, d), lambda l, j: (0, 0)),
                  pl.BlockSpec((1, d, bn), lambda l, j: (l, 0, j)),
                  pl.BlockSpec((1, 1, bn), lambda l, j: (l, 0, j))],
        out_specs=pl.BlockSpec((1, b, bn), lambda l, j: (l, 0, j)),
        out_shape=jax.ShapeDtypeStruct((depth, b, n), F32),
        compiler_params=_params(("parallel", "parallel"), vmem),
        name="ada_mod",
    )(c, ada_w, ada_b.reshape(depth, 1, n))


def _attn_proj_kernel(x_ref, g_ref, sc_ref, sh_ref, w_ref, o_ref, lhs_ref, *maybe_hs, d, bm, bn, rc):
    j = pl.program_id(2)
    rpc = bm // d
    nlc = x_ref.shape[-1] // V7X_LANES

    @pl.when(j == 0)
    def _():
        g, sc, sh = g_ref[...], sc_ref[0], sh_ref[0]
        for c in range(bm // rc):
            rows = pl.ds(c * rc, rc)
            h = _norm_mod(x_ref[0, rows, :], g, sc, sh)
            if d == 1:
                lhs_ref[rows, :] = h.astype(BF16)
            else:
                for lc in range(nlc):
                    maybe_hs[0][lc, rows, :] = h[:, lc * V7X_LANES:(lc + 1) * V7X_LANES]
        if d > 1:
            hs_ref = maybe_hs[0]
            for r in range(d):
                for lc in range(nlc):
                    lhs_ref[pl.ds(r * rpc, rpc), pl.ds(lc * V7X_LANES, V7X_LANES)] = (
                        hs_ref[lc, pl.ds(r, rpc, stride=d), :].astype(BF16))

    acc = jnp.dot(lhs_ref[...], w_ref[...], preferred_element_type=F32)
    for hh in range(bn // HEAD_DIM):
        for r in range(d):
            o_ref[0, hh, r] = acc[r * rpc:(r + 1) * rpc,
                                  hh * HEAD_DIM:(hh + 1) * HEAD_DIM].astype(BF16)


def _attn_proj(x, g, sc, sh, w, d, bm=1024, bn=1024):
    b, t, dm = x.shape
    n = w.shape[1]
    nh = n // HEAD_DIM
    rpc = bm // d
    rc = 256
    assert rpc % V7X_BF16_SUBLANES == 0 and bm % rc == 0 and t % bm == 0 and n % bn == 0
    scratch = [pltpu.VMEM((bm, dm), BF16)]
    if d > 1:
        scratch.append(pltpu.VMEM((dm // V7X_LANES, bm, V7X_LANES), F32))
    vmem = (2 * bm * dm * 4 + bm * dm * 2 + (d > 1) * bm * dm * 4 + 2 * dm * bn * 2
            + 2 * bm * bn * 2 + bm * bn * 4 + 4 * rc * dm * 4 + 4 * MIB)
    out = pl.pallas_call(
        functools.partial(_attn_proj_kernel, d=d, bm=bm, bn=bn, rc=rc),
        grid=(b, t // bm, n // bn),
        in_specs=[pl.BlockSpec((1, bm, dm), lambda bi, i, j: (bi, i, 0)),
                  pl.BlockSpec((1, dm), lambda bi, i, j: (0, 0)),
                  pl.BlockSpec((1, 1, dm), lambda bi, i, j: (bi, 0, 0)),
                  pl.BlockSpec((1, 1, dm), lambda bi, i, j: (bi, 0, 0)),
                  pl.BlockSpec((dm, bn), lambda bi, i, j: (0, j))],
        out_specs=pl.BlockSpec((1, bn // HEAD_DIM, d, rpc, HEAD_DIM),
                               lambda bi, i, j: (bi, j, 0, i, 0)),
        out_shape=jax.ShapeDtypeStruct((b, nh, d, t // d, HEAD_DIM), BF16),
        scratch_shapes=scratch,
        compiler_params=_params(("parallel", "parallel", "arbitrary"), vmem),
        name=f"attn_proj_d{d}",
    )(x, g, sc, sh, w)
    return out.reshape(b, nh, t, HEAD_DIM)


def _attn_kernel(slopes_ref, q1, k1, v1, q4, k4, v4, q16, k16, v16, o_ref, o_scr, l_scr, *, t):
    blk = ATTN_BLOCK
    slope = slopes_ref[pl.program_id(1)]
    scale = HEAD_DIM ** -0.5
    qi = lax.broadcasted_iota(jnp.int32, (blk, 2 * blk), 0)
    kj = lax.broadcasted_iota(jnp.int32, (blk, 2 * blk), 1)
    dist = blk + qi - kj
    in_band = (dist >= 0) & (dist <= blk)

    refs = ((q1, k1, v1), (q4, k4, v4), (q16, k16, v16))
    for gi, ((_, d), (q_ref, k_ref, v_ref)) in enumerate(zip(DILATED_GROUPS, refs)):
        cls_len = t // d
        nb = cls_len // blk
        bias_full = jnp.where(in_band, -slope * (dist * d).astype(F32), -jnp.inf)
        bias_first = bias_full[:, blk:]

        def class_body(r, carry, d=d, nb=nb, cls_len=cls_len, gi=gi, q_ref=q_ref, k_ref=k_ref,
                       v_ref=v_ref, bias_full=bias_full, bias_first=bias_first):
            base = 0 if d == 1 else pl.multiple_of(r * cls_len, blk)
            for n in range(nb):
                q = q_ref[0, 0, pl.ds(base + n * blk, blk), :]
                if n == 0:
                    kk = k_ref[0, 0, pl.ds(base, blk), :]
                    vv = v_ref[0, 0, pl.ds(base, blk), :]
                    bias = bias_first
                else:
                    kk = k_ref[0, 0, pl.ds(base + (n - 1) * blk, 2 * blk), :]
                    vv = v_ref[0, 0, pl.ds(base + (n - 1) * blk, 2 * blk), :]
                    bias = bias_full
                s = lax.dot_general(q, kk, (((1,), (1,)), ((), ())), preferred_element_type=F32)
                s = s * scale + bias
                m = jnp.max(s, axis=-1, keepdims=True)
                e = jnp.exp(s - m)
                den = jnp.sum(e, axis=-1, keepdims=True)
                p = e / den
                o = jnp.dot(p.astype(BF16), vv, preferred_element_type=F32)
                lse = jnp.broadcast_to(m + jnp.log(den), (blk, HEAD_DIM))
                if d == 1:
                    rows = pl.ds(n * blk, blk)
                else:
                    rows = pl.ds(n * blk * d + r, blk, stride=d)
                o_scr[gi, rows, :] = o
                l_scr[gi, rows, :] = lse
            return carry

        if d == 1:
            class_body(0, 0)
        else:
            lax.fori_loop(0, d, class_body, 0)

    rc = 256

    def combine(c, carry):
        rows = pl.ds(pl.multiple_of(c * rc, rc), rc)
        l0, l1, l2 = l_scr[0, rows, :], l_scr[1, rows, :], l_scr[2, rows, :]
        mx = jnp.maximum(jnp.maximum(l0, l1), l2)
        w0, w1, w2 = jnp.exp(l0 - mx), jnp.exp(l1 - mx), jnp.exp(l2 - mx)
        inv = 1.0 / (w0 + w1 + w2)
        o = (w0 * inv) * o_scr[0, rows, :] + (w1 * inv) * o_scr[1, rows, :] \
            + (w2 * inv) * o_scr[2, rows, :]
        o_ref[0, rows, :] = o.astype(BF16)
        return carry

    lax.fori_loop(0, t // rc, combine, 0)


def _attention(qkv, slopes):
    b, _, t, dh = qkv[0].shape
    h = N_HEADS
    in_specs = [pl.BlockSpec(memory_space=pltpu.SMEM)]
    for _ in qkv:
        for part in range(3):
            in_specs.append(pl.BlockSpec((1, 1, t, dh),
                                         functools.partial(lambda bi, hi, part: (bi, part * N_HEADS + hi, 0, 0),
                                                           part=part)))
    vmem = 2 * 9 * t * dh * 2 + 2 * t * dh * 2 + 6 * t * dh * 4 + 8 * MIB
    args = [slopes]
    for a in qkv:
        args += [a, a, a]
    return pl.pallas_call(
        functools.partial(_attn_kernel, t=t),
        grid=(b, h),
        in_specs=in_specs,
        out_specs=pl.BlockSpec((1, t, dh), lambda bi, hi: (bi, 0, hi)),
        out_shape=jax.ShapeDtypeStruct((b, t, h * dh), BF16),
        scratch_shapes=[pltpu.VMEM((3, t, dh), F32), pltpu.VMEM((3, t, dh), F32)],
        compiler_params=_params(("parallel", "parallel"), vmem),
        name="dilated_attn",
    )(*args)


def _mm_res_kernel(*refs, nk, tb_rows, final, rc):
    fg_ref = accs_ref = None
    if final:
        lhs_ref, w_ref, x_ref, gate_ref, fg_ref, o_ref, acc_ref = refs
    elif tb_rows is not None:
        lhs_ref, w_ref, x_ref, gate_ref, o_ref, acc_ref, accs_ref = refs
    else:
        lhs_ref, w_ref, x_ref, gate_ref, o_ref, acc_ref = refs
    k = pl.program_id(1)
    part = jnp.dot(lhs_ref[...], w_ref[...], preferred_element_type=F32)

    if nk == 1:
        acc_ref[...] = part
    else:
        @pl.when(k == 0)
        def _():
            acc_ref[...] = part

        @pl.when(k > 0)
        def _():
            acc_ref[...] += part

    def finish(xs, gate, res):
        o = xs + gate * res
        if final:
            ms = jnp.mean(o * o, axis=-1, keepdims=True)
            o = (o * lax.rsqrt(ms + EPS)) * fg_ref[...]
        return o

    @pl.when(k == nk - 1)
    def _():
        if tb_rows is None:
            gate = gate_ref[0]
            bm = acc_ref.shape[0]
            for c in range(bm // rc):
                rows = pl.ds(c * rc, rc)
                o_ref[rows, :] = finish(x_ref[rows, :], gate, acc_ref[rows, :])
        else:
            nb, bt = tb_rows
            nlc = acc_ref.shape[-1] // V7X_LANES
            for lc in range(nlc):
                accs_ref[lc] = acc_ref[:, pl.ds(lc * V7X_LANES, V7X_LANES)]
            for bi in range(nb):
                for lc in range(nlc):
                    lanes = pl.ds(lc * V7X_LANES, V7X_LANES)
                    res = accs_ref[lc, pl.ds(bi, bt, stride=nb), :]
                    o_ref[bi, :, lanes] = finish(x_ref[bi, :, lanes], gate_ref[bi, :, lanes], res)


def _mm_res(lhs, w, x, gate, *, bm, bk, time_batch=False, final_g=None):
    b, t, dm = x.shape
    m, kdim = lhs.shape
    assert m == b * t and kdim % bk == 0 and m % bm == 0
    nk = kdim // bk
    final = final_g is not None
    if time_batch:
        bt = bm // b
        x_in, x_spec = x, pl.BlockSpec((b, bt, dm), lambda i, k: (0, i, 0))
        gate_spec = pl.BlockSpec((b, 1, dm), lambda i, k: (0, 0, 0))
        out_shape = jax.ShapeDtypeStruct((b, t, dm), F32)
        tb_rows = (b, bt)
    else:
        assert t % bm == 0
        x_in, x_spec = x.reshape(m, dm), pl.BlockSpec((bm, dm), lambda i, k: (i, 0))
        gate_spec = pl.BlockSpec((1, 1, dm), lambda i, k: (i * bm // t, 0, 0))
        out_shape = jax.ShapeDtypeStruct((m, dm), F32)
        tb_rows = None
    in_specs = [pl.BlockSpec((bm, bk), lambda i, k: (i, k)),
                pl.BlockSpec((bk, dm), lambda i, k: (k, 0)),
                x_spec, gate_spec]
    args = [lhs, w, x_in, gate]
    if final:
        in_specs.append(pl.BlockSpec((1, dm), lambda i, k: (0, 0)))
        args.append(final_g)
    rc = min(bm, 256)
    scratch = [pltpu.VMEM((bm, dm), F32)]
    if time_batch:
        assert not final
        scratch.append(pltpu.VMEM((dm // V7X_LANES, bm, V7X_LANES), F32))
    vmem = (2 * bm * bk * 2 + 2 * bk * dm * 2 + 4 * bm * dm * 4 + (2 + time_batch) * bm * dm * 4
            + 6 * rc * dm * 4 + 4 * MIB)
    out = pl.pallas_call(
        functools.partial(_mm_res_kernel, nk=nk, tb_rows=tb_rows, final=final, rc=rc),
        grid=(m // bm, nk),
        in_specs=in_specs,
        out_specs=x_spec,
        out_shape=out_shape,
        scratch_shapes=scratch,
        compiler_params=_params(("parallel", "arbitrary"), vmem),
        name="mm_res_tb" if time_batch else ("mm_res_final" if final else "mm_res"),
    )(*args)
    return out.reshape(b, t, dm)


def _ffn_up_kernel(x_ref, xh_ref, g_ref, sc_ref, sh_ref, wg_ref, wu_ref, cwg_ref, cwu_ref,
                   cbg_ref, cbu_ref, o_ref, lhs_ref, *, bm, rc):
    i, j = pl.program_id(1), pl.program_id(2)

    @pl.when(j == 0)
    def _():
        g, sc, sh = g_ref[...], sc_ref[0], sh_ref[0]
        halo = _norm_mod(xh_ref[0], g, sc, sh)
        lhs_ref[pl.ds(0, CONV_HALO), :] = jnp.where(i == 0, 0.0, halo).astype(BF16)
        for c in range(bm // rc):
            lhs_ref[pl.ds(CONV_HALO + c * rc, rc), :] = _norm_mod(
                x_ref[0, pl.ds(c * rc, rc), :], g, sc, sh).astype(BF16)

    lhs = lhs_ref[...]

    def conv_branch(w_ref, cw_ref, cb_ref):
        a = jnp.dot(lhs, w_ref[...], preferred_element_type=F32)
        a1 = pltpu.roll(a, 1, axis=0)
        a2 = pltpu.roll(a, 2, axis=0)
        cw = cw_ref[...]
        return (cb_ref[...] + cw[0:1] * a2[CONV_HALO:] + cw[1:2] * a1[CONV_HALO:]
                + cw[2:3] * a[CONV_HALO:])

    gate = conv_branch(wg_ref, cwg_ref, cbg_ref)
    up = conv_branch(wu_ref, cwu_ref, cbu_ref)
    o_ref[0] = (gate * _sigmoid(gate) * up).astype(BF16)


def _ffn_up(x, g, sc, sh, w_up, conv_w, conv_b, bm=1024, bn=512):
    b, t, dm = x.shape
    f = w_up.shape[1] // 2
    assert t % bm == 0 and f % bn == 0 and bm % CONV_HALO == 0
    nj = f // bn
    rc = 256
    vmem = (2 * bm * dm * 4 + (bm + CONV_HALO) * dm * 2 + 4 * dm * bn * 2 + 2 * bm * bn * 2
            + 8 * (bm + CONV_HALO) * bn * 4 + 4 * rc * dm * 4 + 4 * MIB)
    halo_blocks = bm // CONV_HALO
    return pl.pallas_call(
        functools.partial(_ffn_up_kernel, bm=bm, rc=rc),
        grid=(b, t // bm, nj),
        in_specs=[pl.BlockSpec((1, bm, dm), lambda bi, i, j: (bi, i, 0)),
                  pl.BlockSpec((1, CONV_HALO, dm),
                               lambda bi, i, j: (bi, jnp.maximum(i * halo_blocks - 1, 0), 0)),
                  pl.BlockSpec((1, dm), lambda bi, i, j: (0, 0)),
                  pl.BlockSpec((1, 1, dm), lambda bi, i, j: (bi, 0, 0)),
                  pl.BlockSpec((1, 1, dm), lambda bi, i, j: (bi, 0, 0)),
                  pl.BlockSpec((dm, bn), lambda bi, i, j: (0, j)),
                  pl.BlockSpec((dm, bn), lambda bi, i, j: (0, nj + j)),
                  pl.BlockSpec((CONV_W, bn), lambda bi, i, j: (0, j)),
                  pl.BlockSpec((CONV_W, bn), lambda bi, i, j: (0, nj + j)),
                  pl.BlockSpec((1, bn), lambda bi, i, j: (0, j)),
                  pl.BlockSpec((1, bn), lambda bi, i, j: (0, nj + j))],
        out_specs=pl.BlockSpec((1, bm, bn), lambda bi, i, j: (bi, i, j)),
        out_shape=jax.ShapeDtypeStruct((b, t, f), BF16),
        scratch_shapes=[pltpu.VMEM((bm + CONV_HALO, dm), BF16)],
        compiler_params=_params(("parallel", "parallel", "arbitrary"), vmem),
        name="ffn_up",
    )(x, x, g, sc, sh, w_up, w_up, conv_w, conv_w, conv_b, conv_b)


def _ssm_disc_kernel(ls_ref, ar_ref, ai_ref, br_ref, bi_ref, lr_ref, li_ref, bbr_ref, bbi_ref):
    step = jnp.exp(ls_ref[...])
    lr, li = ar_ref[...], ai_ref[...]
    mag = jnp.exp(lr * step)
    ang = li * step
    lb_re = mag * jnp.cos(ang)
    lb_im = mag * jnp.sin(ang)
    nr, ni = lb_re - 1.0, lb_im
    den = lr * lr + li * li
    coef_re = (nr * lr + ni * li) / den
    coef_im = (ni * lr - nr * li) / den
    br, bi = br_ref[...], bi_ref[...]
    lr_ref[...] = lb_re
    li_ref[...] = lb_im
    bbr_ref[...] = coef_re * br - coef_im * bi
    bbi_ref[...] = coef_re * bi + coef_im * br


def _ssm_discretize(log_step, a_re, a_im, b_re, b_im, bg=SSM_CHUNK_GROUPS):
    g, p, c = b_re.shape
    col = jax.ShapeDtypeStruct((g, p, 1), F32)
    full = jax.ShapeDtypeStruct((g, p, c), F32)
    col_spec = pl.BlockSpec((bg, p, 1), lambda i: (i, 0, 0))
    full_spec = pl.BlockSpec((bg, p, c), lambda i: (i, 0, 0))
    return pl.pallas_call(
        _ssm_disc_kernel,
        grid=(g // bg,),
        in_specs=[pl.BlockSpec((bg, 1, 1), lambda i: (i, 0, 0)), col_spec, col_spec,
                  full_spec, full_spec],
        out_specs=(col_spec, col_spec, full_spec, full_spec),
        out_shape=(col, col, full, full),
        compiler_params=_params(("parallel",), 32 * MIB),
        name="ssm_discretize",
    )(log_step.reshape(g, 1, 1), a_re.reshape(g, p, 1), a_im.reshape(g, p, 1), b_re, b_im)


def _ssm_in_kernel(x_ref, g_ref, sc_ref, sh_ref, w_ref, o_ref, hs_ref, lhs_ref, *, nb, bt):
    g = g_ref[...]
    nlc = x_ref.shape[-1] // V7X_LANES
    for bi in range(nb):
        h = _norm_mod(x_ref[bi], g, sc_ref[bi], sh_ref[bi])
        for lc in range(nlc):
            hs_ref[lc, pl.ds(bi, bt, stride=nb), :] = h[:, lc * V7X_LANES:(lc + 1) * V7X_LANES]
    for lc in range(nlc):
        lhs_ref[:, pl.ds(lc * V7X_LANES, V7X_LANES)] = hs_ref[lc].astype(BF16)
    o_ref[...] = jnp.dot(lhs_ref[...], w_ref[...], preferred_element_type=F32)


def _ssm_in(x, g, sc, sh, w, bt=32):
    b, t, dm = x.shape
    n = w.shape[1]
    bm = bt * b
    vmem = (2 * bm * dm * 4 + bm * dm * 4 + bm * dm * 2 + 2 * dm * n * 2 + 2 * bm * n * 4
            + bm * n * 4 + 4 * MIB)
    return pl.pallas_call(
        functools.partial(_ssm_in_kernel, nb=b, bt=bt),
        grid=(t // bt,),
        in_specs=[pl.BlockSpec((b, bt, dm), lambda i: (0, i, 0)),
                  pl.BlockSpec((1, dm), lambda i: (0, 0)),
                  pl.BlockSpec((b, 1, dm), lambda i: (0, 0, 0)),
                  pl.BlockSpec((b, 1, dm), lambda i: (0, 0, 0)),
                  pl.BlockSpec((dm, n), lambda i: (0, 0))],
        out_specs=pl.BlockSpec((bm, n), lambda i: (i, 0)),
        out_shape=jax.ShapeDtypeStruct((t * b, n), F32),
        scratch_shapes=[pltpu.VMEM((dm // V7X_LANES, bm, V7X_LANES), F32),
                        pltpu.VMEM((bm, dm), BF16)],
        compiler_params=_params(("parallel",), vmem),
        name="ssm_in_proj",
    )(x, g, sc, sh, w)


def _gelu_tanh(y):
    return 0.5 * y * (1.0 + jnp.tanh(math.sqrt(2.0 / math.pi) * (y + 0.044715 * (y * y * y))))


def _ssm_core_kernel(u_ref, bmat_ref, lr_ref, li_ref, cre_ref, cim_ref, dsk_ref, o_ref,
                     st_ref, buf_ref, *, nb, rb, rc):
    s = SSM_CHUNK_STATES
    hw = SSM_SCAN_LANES

    @pl.when(pl.program_id(1) == 0)
    def _():
        st_ref[...] = jnp.zeros_like(st_ref)

    for c in range(rb // rc):
        rows = pl.ds(c * rc, rc)
        buf_ref[rows, :] = jnp.dot(u_ref[rows, :].astype(BF16), bmat_ref[0],
                                   preferred_element_type=F32)

    for half in range(s // hw):
        cr = pl.ds(half * hw, hw)
        ci = pl.ds(s + half * hw, hw)
        lr = jnp.broadcast_to(lr_ref[0, :, cr], (nb, hw))
        li = jnp.broadcast_to(li_ref[0, :, cr], (nb, hw))

        def step(tt, carry, cr=cr, ci=ci, lr=lr, li=li):
            xr, xi = carry
            rows = pl.ds(pl.multiple_of(tt * nb, nb), nb)
            nxr = lr * xr - li * xi + buf_ref[rows, cr]
            nxi = lr * xi + li * xr + buf_ref[rows, ci]
            buf_ref[rows, cr] = nxr
            buf_ref[rows, ci] = nxi
            return nxr, nxi

        xr, xi = lax.fori_loop(0, rb // nb, step, (st_ref[:, cr], st_ref[:, ci]), unroll=4)
        st_ref[:, cr] = xr
        st_ref[:, ci] = xi

    for c in range(rb // rc):
        rows = pl.ds(c * rc, rc)
        xb = buf_ref[rows, :].astype(BF16)
        y = (jnp.dot(xb[:, :s], cre_ref[0], preferred_element_type=F32)
             - jnp.dot(xb[:, s:], cim_ref[0], preferred_element_type=F32))
        y = y + dsk_ref[0] * u_ref[rows, :]
        o_ref[rows, :] = _gelu_tanh(y).astype(BF16)


def _ssm_core(u, bmat, lam_re, lam_im, cre, cim, d_skip, nb, rb=1024):
    m, dm = u.shape
    nchunk = dm // SSM_CHUNK
    s = SSM_CHUNK_STATES
    rc = 256
    vmem = (2 * rb * SSM_CHUNK * 4 + 2 * SSM_CHUNK * 2 * s * 2 + 4 * s * SSM_CHUNK * 2
            + 2 * rb * SSM_CHUNK * 2 + rb * 2 * s * 4 + nb * 2 * s * 4
            + 6 * rc * 2 * s * 4 + 4 * MIB)
    return pl.pallas_call(
        functools.partial(_ssm_core_kernel, nb=nb, rb=rb, rc=rc),
        grid=(nchunk, m // rb),
        in_specs=[pl.BlockSpec((rb, SSM_CHUNK), lambda c, i: (i, c)),
                  pl.BlockSpec((1, SSM_CHUNK, 2 * s), lambda c, i: (c, 0, 0)),
                  pl.BlockSpec((1, 1, s), lambda c, i: (c, 0, 0)),
                  pl.BlockSpec((1, 1, s), lambda c, i: (c, 0, 0)),
                  pl.BlockSpec((1, s, SSM_CHUNK), lambda c, i: (c, 0, 0)),
                  pl.BlockSpec((1, s, SSM_CHUNK), lambda c, i: (c, 0, 0)),
                  pl.BlockSpec((1, 1, SSM_CHUNK), lambda c, i: (c, 0, 0))],
        out_specs=pl.BlockSpec((rb, SSM_CHUNK), lambda c, i: (i, c)),
        out_shape=jax.ShapeDtypeStruct((m, dm), BF16),
        scratch_shapes=[pltpu.VMEM((nb, 2 * s), F32), pltpu.VMEM((rb, 2 * s), F32)],
        compiler_params=_params(("parallel", "arbitrary"), vmem),
        name="ssm_core",
    )(u, bmat, lam_re, lam_im, cre, cim, d_skip)


def _ssm_gate_kernel(g_ref, w_ref, o_ref):
    g = g_ref[...]
    z = jnp.dot(g, w_ref[...], preferred_element_type=F32)
    o_ref[...] = (g.astype(F32) * _sigmoid(z)).astype(BF16)


def _ssm_gate(g, w, bm=512):
    m, dm = g.shape
    vmem = 4 * bm * dm * 2 + 2 * dm * dm * 2 + 4 * bm * dm * 4 + 4 * MIB
    return pl.pallas_call(
        _ssm_gate_kernel,
        grid=(m // bm,),
        in_specs=[pl.BlockSpec((bm, dm), lambda i: (i, 0)),
                  pl.BlockSpec((dm, dm), lambda i: (0, 0))],
        out_specs=pl.BlockSpec((bm, dm), lambda i: (i, 0)),
        out_shape=jax.ShapeDtypeStruct((m, dm), BF16),
        compiler_params=_params(("parallel",), vmem),
        name="ssm_glu",
    )(g, w)


def _block_diag_chunks(a):
    g, r, s = a.shape
    n = SSM_CHUNK_GROUPS
    eye = jnp.eye(n, dtype=a.dtype)
    out = a.reshape(g // n, n, r, 1, s) * eye.reshape(1, n, 1, n, 1)
    return out.reshape(g // n, n * r, n * s)


def _s5_mixer(x, g, sc, sh, gate, w_in, log_step, a_re, a_im, b_re, b_im, c_re, c_im, d_skip,
              w_gate, w_out):
    b, t, dm = x.shape
    ng = dm // SSM_GROUP
    nchunk = dm // SSM_CHUNK
    s = SSM_CHUNK_STATES
    lam_re, lam_im, bb_re, bb_im = _ssm_discretize(log_step, a_re, a_im, b_re, b_im)
    bmat = jnp.concatenate([_block_diag_chunks(bb_re.transpose(0, 2, 1)),
                            _block_diag_chunks(bb_im.transpose(0, 2, 1))], axis=-1).astype(BF16)
    cre = _block_diag_chunks(c_re.transpose(0, 2, 1)).astype(BF16)
    cim = _block_diag_chunks(c_im.transpose(0, 2, 1)).astype(BF16)
    lam_re = lam_re.reshape(nchunk, 1, s)
    lam_im = lam_im.reshape(nchunk, 1, s)
    u = _ssm_in(x, g, sc, sh, w_in)
    gq = _ssm_core(u, bmat, lam_re, lam_im, cre, cim, d_skip.reshape(nchunk, 1, SSM_CHUNK), nb=b)
    gq = _ssm_gate(gq, w_gate)
    return _mm_res(gq, w_out, x, gate, bm=32 * b, bk=dm, time_batch=True)


def kernel(x, c, ada_w, ada_b, norm1_g, norm2_g, attn_w_in, attn_w_out, ssm_w_in, ssm_log_step, ssm_a_re, ssm_a_im, ssm_b_re, ssm_b_im, ssm_c_re, ssm_c_im, ssm_d, ssm_w_gate, ssm_w_out, ffn_w_up, ffn_conv_w, ffn_conv_b, ffn_w_down, final_norm_g):
    b, t, dm = x.shape
    depth = ada_w.shape[0]
    hd = N_HEADS * HEAD_DIM
    n_groups = len(DILATED_GROUPS)
    qk_cols = n_groups * hd
    slopes = 2.0 ** (-8.0 * jnp.arange(1, N_HEADS + 1, dtype=F32) / N_HEADS)

    mod = _ada(c, ada_w, ada_b).reshape(depth, b, 6, 1, dm)
    for i in range(depth):
        j = i // 2
        sh1, sc1, g1, sh2, sc2, g2 = (mod[i, :, q] for q in range(6))
        n1 = norm1_g[i].reshape(1, dm)
        if i % 2 == 0:
            w_in = attn_w_in[j].astype(BF16)
            wv = w_in[:, 2 * qk_cols:]
            qkv = []
            for gi, (_, d) in enumerate(DILATED_GROUPS):
                wq = w_in[:, gi * hd:(gi + 1) * hd]
                wk = w_in[:, qk_cols + gi * hd:qk_cols + (gi + 1) * hd]
                qkv.append(_attn_proj(x, n1, sc1, sh1, jnp.concatenate([wq, wk, wv], axis=1), d))
            o = _attention(qkv, slopes)
            x = _mm_res(o.reshape(b * t, hd), attn_w_out[j].astype(BF16), x, g1, bm=512, bk=hd)
        else:
            x = _s5_mixer(x, n1, sc1, sh1, g1, ssm_w_in[j].astype(BF16), ssm_log_step[j],
                          ssm_a_re[j], ssm_a_im[j], ssm_b_re[j], ssm_b_im[j], ssm_c_re[j],
                          ssm_c_im[j], ssm_d[j], ssm_w_gate[j].astype(BF16),
                          ssm_w_out[j].astype(BF16))
        a = _ffn_up(x, norm2_g[i].reshape(1, dm), sc2, sh2, ffn_w_up[i].astype(BF16),
                    ffn_conv_w[i], ffn_conv_b[i].reshape(1, 2 * D_FF))
        x = _mm_res(a.reshape(b * t, D_FF), ffn_w_down[i].astype(BF16), x, g2, bm=512,
                    bk=D_FF // 4,
                    final_g=final_norm_g.reshape(1, dm) if i == depth - 1 else None)
    return x
```

```python
import functools
import math

import jax
import jax.numpy as jnp
from jax import lax
from jax.experimental import pallas as pl
from jax.experimental.pallas import tpu as pltpu

F32 = jnp.float32
BF16 = jnp.bfloat16

D_MODEL = 2048
N_HEADS = 16
HEAD_DIM = 128
DILATED_GROUPS = ((128, 1), (512, 4), (2048, 16))
ATTN_BLOCK = 128
SSM_GROUP = 16
SSM_STATE = 64
D_FF = 5632
CONV_W = 3
EPS = 1e-6

V7X_LANES = 128
V7X_BF16_SUBLANES = 16
V7X_MXU_COLS = 256
SAFE_STRIDE = 4
V7X_SCOPED_VMEM_CAP_BYTES = 60000 * 1024
MIB = 1024 * 1024

SSM_CHUNK = 256
SSM_CHUNK_GROUPS = SSM_CHUNK // SSM_GROUP
SSM_CHUNK_STATES = SSM_CHUNK_GROUPS * SSM_STATE
SSM_SCAN_LANES = 512
LHS_K_CHUNKS = (512, 1536)
CONV_HALO = 16


def _params(semantics, vmem_bytes):
    return pltpu.CompilerParams(
        dimension_semantics=semantics,
        vmem_limit_bytes=int(min(vmem_bytes, V7X_SCOPED_VMEM_CAP_BYTES)))


def _sigmoid(x):
    return 1.0 / (1.0 + jnp.exp(-x))


def _norm_mod(x, g, sc, sh):
    ms = jnp.mean(x * x, axis=-1, keepdims=True)
    y = x * lax.rsqrt(ms + EPS)
    return (y * g) * (1.0 + sc) + sh


def _ada_kernel(c_ref, w_ref, b_ref, o_ref):
    c = c_ref[...]
    cs = (c * _sigmoid(c)).astype(BF16)
    o_ref[0] = jnp.dot(cs, w_ref[0].astype(BF16), preferred_element_type=F32) + b_ref[0]


def _ada(c, ada_w, ada_b, bn=1024):
    depth, d, n = ada_w.shape
    b = c.shape[0]
    vmem = 2 * (d * bn * 4) + d * bn * 2 + 4 * b * (d + bn) * 4 + 4 * MIB
    return pl.pallas_call(
        _ada_kernel,
        grid=(depth, n // bn),
        in_specs=[pl.BlockSpec((b, d), lambda l, j: (0, 0)),
                  pl.BlockSpec((1, d, bn), lambda l, j: (l, 0, j)),
                  pl.BlockSpec((1, 1, bn), lambda l, j: (l, 0, j))],
        out_specs=pl.BlockSpec((1, b, bn), lambda l, j: (l, 0, j)),
        out_shape=jax.ShapeDtypeStruct((depth, b, n), F32),
        compiler_params=_params(("parallel", "parallel"), vmem),
        name="ada_mod",
    )(c, ada_w, ada_b.reshape(depth, 1, n))


def _gather_classes(stage_ref, stage2_ref, chunk, nrows, dils, emit):
    quarter = nrows // SAFE_STRIDE
    for r0 in range(SAFE_STRIDE):
        rows = stage_ref[chunk, pl.ds(r0, quarter, stride=SAFE_STRIDE), :]
        if SAFE_STRIDE in dils:
            emit(SAFE_STRIDE, r0, rows)
        if any(d > SAFE_STRIDE for d in dils):
            stage2_ref[chunk, pl.ds(r0 * quarter, quarter), :] = rows
    for d in dils:
        if d > SAFE_STRIDE:
            step = d // SAFE_STRIDE
            assert step <= SAFE_STRIDE
            for r0 in range(SAFE_STRIDE):
                for r1 in range(step):
                    emit(d, SAFE_STRIDE * r1 + r0,
                         stage2_ref[chunk, pl.ds(r0 * quarter + r1, nrows // d, stride=step), :])


def _scatter_class(stage2_ref, chunk, nrows, d, r, rows):
    step = d // SAFE_STRIDE
    assert SAFE_STRIDE < d and step <= SAFE_STRIDE
    r0, r1 = r % SAFE_STRIDE, r // SAFE_STRIDE
    stage2_ref[chunk, pl.ds(r0 * (nrows // SAFE_STRIDE) + r1, nrows // d, stride=step), :] = rows


def _scatter_finish(stage_ref, stage2_ref, chunk, nrows):
    quarter = nrows // SAFE_STRIDE
    for r0 in range(SAFE_STRIDE):
        stage_ref[chunk, pl.ds(r0, quarter, stride=SAFE_STRIDE), :] = (
            stage2_ref[chunk, pl.ds(r0 * quarter, quarter), :])


def _attn_proj_kernel(x_ref, g_ref, sc_ref, sh_ref, w_ref, o_ref, *rest, d, bm, bn, rc, kc, v_dils,
                      v_first_block):
    nv = len(v_dils)
    v_refs, lhs_ref, inv_ref, stage_ref = rest[:nv], rest[nv], rest[nv + 1], rest[nv + 2]
    stage2_ref = rest[nv + 3] if len(rest) > nv + 3 else None
    j = pl.program_id(2)
    rpc = bm // d
    dm = x_ref.shape[-1]
    heads = bn // HEAD_DIM

    def write_out(acc):
        for hh in range(heads):
            for r in range(d):
                o_ref[0, hh, r] = acc[r * rpc:(r + 1) * rpc,
                                      hh * HEAD_DIM:(hh + 1) * HEAD_DIM].astype(BF16)

    @pl.when(j == 0)
    def _():
        g, sc, sh = g_ref[...], sc_ref[0], sh_ref[0]
        for c in range(bm // rc):
            rows = pl.ds(c * rc, rc)
            x = x_ref[0, rows, :]
            inv = lax.rsqrt(jnp.mean(x * x, axis=-1, keepdims=True) + EPS)
            inv_ref[rows, :] = jnp.broadcast_to(inv, (rc, V7X_LANES))
        acc = None
        assert sum(kc) == dm
        for k0, kw in zip((sum(kc[:n]) for n in range(len(kc))), kc):
            for lc in range(k0 // V7X_LANES, (k0 + kw) // V7X_LANES):
                lanes = pl.ds(lc * V7X_LANES, V7X_LANES)
                lo, hi = lc * V7X_LANES, (lc + 1) * V7X_LANES
                for c in range(bm // rc):
                    rows = pl.ds(c * rc, rc)
                    h = (((x_ref[0, rows, lanes] * inv_ref[rows, :]) * g[:, lo:hi])
                         * (1.0 + sc[:, lo:hi]) + sh[:, lo:hi])
                    if d == 1:
                        lhs_ref[rows, lanes] = h.astype(BF16)
                    else:
                        stage_ref[lc, rows, :] = h
                if d > 1:
                    def to_lhs(_, r, rows, lanes=lanes):
                        lhs_ref[pl.ds(r * rpc, rpc), lanes] = rows.astype(BF16)
                    _gather_classes(stage_ref, stage2_ref, lc, bm, (d,), to_lhs)
            klanes = pl.ds(k0, kw)
            part = jnp.dot(lhs_ref[:, klanes], w_ref[klanes, :], preferred_element_type=F32)
            acc = part if acc is None else acc + part
        write_out(acc)

    @pl.when(j > 0)
    def _():
        acc = jnp.dot(lhs_ref[...], w_ref[...], preferred_element_type=F32)
        write_out(acc)
        if v_refs:
            @pl.when(j >= v_first_block)
            def _():
                for hh in range(heads):
                    stage_ref[hh] = acc[:, hh * HEAD_DIM:(hh + 1) * HEAD_DIM]
                for hh in range(heads):
                    def to_v(dd, r, rows, hh=hh):
                        v_refs[v_dils.index(dd)][0, hh, r] = rows.astype(BF16)
                    _gather_classes(stage_ref, stage2_ref, hh, bm, v_dils, to_v)


def _attn_proj(x, g, sc, sh, w, gi, d, bm=1024, bn=1024):
    b, t, dm = x.shape
    w, layer = w
    hd = N_HEADS * HEAD_DIM
    parts = 3 if d == 1 else 2
    n = parts * hd
    per_part = hd // bn
    n_groups = len(DILATED_GROUPS)
    part_start = (gi * per_part, (n_groups + gi) * per_part, 2 * n_groups * per_part)
    v_dils = tuple(dd for _, dd in DILATED_GROUPS if dd > 1) if d == 1 else ()
    v_first_block = 2 * per_part

    def w_block(bi, i, j):
        part = j // per_part
        start = jnp.where(part == 0, part_start[0],
                          jnp.where(part == 1, part_start[1], part_start[2]))
        return (layer, 0, start + j % per_part)

    rpc = bm // d
    rc = 256
    heads = bn // HEAD_DIM
    assert rpc % V7X_BF16_SUBLANES == 0 and bm % rc == 0 and t % bm == 0 and hd % bn == 0
    assert all((bm // dd) % V7X_BF16_SUBLANES == 0 for dd in v_dils)
    stage_chunks = heads if d == 1 else dm // V7X_LANES
    n_stages = 2 if max(v_dils + (d,)) > SAFE_STRIDE else 1
    scratch = ([pltpu.VMEM((bm, dm), BF16), pltpu.VMEM((bm, V7X_LANES), F32)]
               + [pltpu.VMEM((stage_chunks, bm, V7X_LANES), F32)] * n_stages)
    out_specs = [pl.BlockSpec((1, heads, d, rpc, HEAD_DIM), lambda bi, i, j: (bi, j, 0, i, 0))]
    out_shape = [jax.ShapeDtypeStruct((b, n // HEAD_DIM, d, t // d, HEAD_DIM), BF16)]
    for dd in v_dils:
        out_specs.append(pl.BlockSpec(
            (1, heads, dd, bm // dd, HEAD_DIM),
            lambda bi, i, j: (bi, jnp.maximum(j - v_first_block, 0), 0, i, 0)))
        out_shape.append(jax.ShapeDtypeStruct((b, N_HEADS, dd, t // dd, HEAD_DIM), BF16))
    vmem = (2 * bm * dm * 4 + bm * dm * 2 + n_stages * stage_chunks * bm * V7X_LANES * 4
            + 2 * dm * bn * 2
            + 2 * (1 + len(v_dils)) * bm * bn * 2 + bm * bn * 4 + 4 * rc * dm * 4 + 4 * MIB)
    outs = pl.pallas_call(
        functools.partial(_attn_proj_kernel, d=d, bm=bm, bn=bn, rc=rc, kc=LHS_K_CHUNKS, v_dils=v_dils,
                          v_first_block=v_first_block),
        grid=(b, t // bm, n // bn),
        in_specs=[pl.BlockSpec((1, bm, dm), lambda bi, i, j: (bi, i, 0)),
                  pl.BlockSpec((1, dm), lambda bi, i, j: (0, 0)),
                  pl.BlockSpec((1, 1, dm), lambda bi, i, j: (bi, 0, 0)),
                  pl.BlockSpec((1, 1, dm), lambda bi, i, j: (bi, 0, 0)),
                  pl.BlockSpec((None, dm, bn), w_block)],
        out_specs=out_specs,
        out_shape=out_shape,
        scratch_shapes=scratch,
        compiler_params=_params(("parallel", "parallel", "arbitrary"), vmem),
        name=f"attn_proj_d{d}",
    )(x, g, sc, sh, w)
    outs = [o.reshape(b, o.shape[1], t, HEAD_DIM) for o in outs]
    return (outs[0], outs[1:]) if d == 1 else outs[0]


def _attn_kernel(slopes_ref, q1, k1, v1, q4, k4, v4, q16, k16, v16, o_ref, o_scr, l_scr, *, t,
                 stage_batch):
    blk = ATTN_BLOCK
    slope = slopes_ref[pl.program_id(1)]
    scale = HEAD_DIM ** -0.5
    qi = lax.broadcasted_iota(jnp.int32, (blk, 2 * blk), 0)
    kj = lax.broadcasted_iota(jnp.int32, (blk, 2 * blk), 1)
    dist = blk + qi - kj
    in_band = (dist >= 0) & (dist <= blk)

    refs = ((q1, k1, v1), (q4, k4, v4), (q16, k16, v16))
    stage_slot = len(DILATED_GROUPS)
    quarter = t // SAFE_STRIDE
    assert sum(d > SAFE_STRIDE for _, d in DILATED_GROUPS) <= 1
    work = []
    for gi, ((_, d), (q_ref, k_ref, v_ref)) in enumerate(zip(DILATED_GROUPS, refs)):
        cls_len = t // d
        nb = cls_len // blk
        bias_full = jnp.where(in_band, -slope * (dist * d).astype(F32), -jnp.inf)
        bias_first = bias_full[:, blk:]

        for r in range(d):
            base = r * cls_len
            for n in range(nb):
                if d == 1:
                    dst = (gi, pl.ds(n * blk, blk))
                elif d <= SAFE_STRIDE:
                    dst = (gi, pl.ds(n * blk * d + r, blk, stride=d))
                else:
                    step = d // SAFE_STRIDE
                    r0, r1 = r % SAFE_STRIDE, r // SAFE_STRIDE
                    dst = (stage_slot, pl.ds(r0 * quarter + step * n * blk + r1, blk, stride=step))
                q_rows = pl.ds(base + n * blk, blk)
                if n == 0:
                    work.append((q_ref, k_ref, v_ref, q_rows, pl.ds(base, blk), bias_first, dst))
                else:
                    work.append((q_ref, k_ref, v_ref, q_rows,
                                 pl.ds(base + (n - 1) * blk, 2 * blk), bias_full, dst))

    for w0 in range(0, len(work), stage_batch):
        batch = work[w0:w0 + stage_batch]
        ss = [lax.dot_general(q_ref[0, 0, qr, :], k_ref[0, 0, kr, :], (((1,), (1,)), ((), ())),
                              preferred_element_type=F32) * scale + bias
              for (q_ref, k_ref, _, qr, kr, bias, _) in batch]
        ms = [jnp.max(s, axis=-1, keepdims=True) for s in ss]
        es = [jnp.exp(s - m) for s, m in zip(ss, ms)]
        dens = [jnp.sum(e, axis=-1, keepdims=True) for e in es]
        ps = [(e / den).astype(BF16) for e, den in zip(es, dens)]
        outs = [jnp.dot(p, v_ref[0, 0, kr, :], preferred_element_type=F32)
                for p, (_, _, v_ref, _, kr, _, _) in zip(ps, batch)]
        for o, m, den, (_, _, _, _, _, _, (slot, rows)) in zip(outs, ms, dens, batch):
            o_scr[slot, rows, :] = o
            l_scr[slot, rows, :] = jnp.broadcast_to(m + jnp.log(den), (blk, HEAD_DIM))

    for gi, (_, d) in enumerate(DILATED_GROUPS):
        if d > SAFE_STRIDE:
            for scr in (o_scr, l_scr):
                for r0 in range(SAFE_STRIDE):
                    for c in range(quarter // blk):
                        scr[gi, pl.ds(r0 + c * blk * SAFE_STRIDE, blk, stride=SAFE_STRIDE), :] = (
                            scr[stage_slot, pl.ds(r0 * quarter + c * blk, blk), :])

    rc = 256

    def combine(c, carry):
        rows = pl.ds(pl.multiple_of(c * rc, rc), rc)
        l0, l1, l2 = l_scr[0, rows, :], l_scr[1, rows, :], l_scr[2, rows, :]
        mx = jnp.maximum(jnp.maximum(l0, l1), l2)
        w0, w1, w2 = jnp.exp(l0 - mx), jnp.exp(l1 - mx), jnp.exp(l2 - mx)
        o = (w0 * o_scr[0, rows, :] + w1 * o_scr[1, rows, :] + w2 * o_scr[2, rows, :]) \
            / (w0 + w1 + w2)
        o_ref[0, rows, :] = o.astype(BF16)
        return carry

    lax.fori_loop(0, t // rc, combine, 0, unroll=2)


def _attention(groups, slopes, stage_batch=16):
    b, _, t, dh = groups[0][0][0].shape
    h = N_HEADS
    in_specs = [pl.BlockSpec(memory_space=pltpu.SMEM)]
    args = [slopes]
    for group in groups:
        for arr, head0 in group:
            in_specs.append(pl.BlockSpec(
                (1, 1, t, dh), functools.partial(lambda bi, hi, head0: (bi, head0 + hi, 0, 0),
                                                 head0=head0)))
            args.append(arr)
    vmem = 2 * 9 * t * dh * 2 + 2 * t * dh * 2 + 8 * t * dh * 4 + 8 * MIB
    return pl.pallas_call(
        functools.partial(_attn_kernel, t=t, stage_batch=stage_batch),
        grid=(b, h),
        in_specs=in_specs,
        out_specs=pl.BlockSpec((1, t, dh), lambda bi, hi: (bi, 0, hi)),
        out_shape=jax.ShapeDtypeStruct((b, t, h * dh), BF16),
        scratch_shapes=[pltpu.VMEM((len(groups) + 1, t, dh), F32)] * 2,
        compiler_params=_params(("parallel", "parallel"), vmem),
        name="dilated_attn",
    )(*args)


def _mm_res_kernel(lhs_ref, w_ref, x_ref, gate_ref, o_ref, *scratch, nk, tb_rows, rc):
    acc_ref = accs_ref = None
    if tb_rows is not None:
        acc_ref, accs_ref, accs2_ref = scratch
    elif nk > 1:
        acc_ref, = scratch
    k = pl.program_id(2)
    part = jnp.dot(lhs_ref[...], w_ref[...], preferred_element_type=F32)

    def finish(xs, gate, res):
        return xs + gate * res

    if acc_ref is None:
        o_ref[...] = finish(x_ref[...], gate_ref[0], part)
        return

    if nk == 1:
        acc_ref[...] = part
    else:
        @pl.when(k == 0)
        def _():
            acc_ref[...] = part

        @pl.when(k > 0)
        def _():
            acc_ref[...] += part

    @pl.when(k == nk - 1)
    def _():
        if tb_rows is None:
            gate = gate_ref[0]
            bm = acc_ref.shape[0]
            for c in range(bm // rc):
                rows = pl.ds(c * rc, rc)
                o_ref[rows, :] = finish(x_ref[rows, :], gate, acc_ref[rows, :])
        else:
            nb, bt = tb_rows
            nlc = acc_ref.shape[-1] // V7X_LANES
            for lc in range(nlc):
                lanes = pl.ds(lc * V7X_LANES, V7X_LANES)
                accs_ref[lc] = acc_ref[:, lanes]

                def put(_, bi, res, lanes=lanes):
                    o_ref[bi, :, lanes] = finish(x_ref[bi, :, lanes], gate_ref[bi, :, lanes], res)

                _gather_classes(accs_ref, accs2_ref, lc, nb * bt, (nb,), put)


def _mm_res(lhs, w, x, gate, *, bm, bk, bn=None, time_batch=False):
    b, t, dm = x.shape
    w, layer = w
    m, kdim = lhs.shape
    bn = dm if bn is None else bn
    assert m == b * t and kdim % bk == 0 and m % bm == 0 and dm % bn == 0
    nk = kdim // bk
    if time_batch:
        assert bn == dm and b > SAFE_STRIDE
        bt = bm // b
        x_in, x_spec = x, pl.BlockSpec((b, bt, dm), lambda j, i, k: (0, i, 0))
        gate_spec = pl.BlockSpec((b, 1, dm), lambda j, i, k: (0, 0, 0))
        out_shape = jax.ShapeDtypeStruct((b, t, dm), F32)
        tb_rows = (b, bt)
        scratch = ([pltpu.VMEM((bm, dm), F32)]
                   + [pltpu.VMEM((dm // V7X_LANES, bm, V7X_LANES), F32)] * 2)
    else:
        assert t % bm == 0
        x_in, x_spec = x.reshape(m, dm), pl.BlockSpec((bm, bn), lambda j, i, k: (i, j))
        gate_spec = pl.BlockSpec((1, 1, bn), lambda j, i, k: (i * bm // t, 0, j))
        out_shape = jax.ShapeDtypeStruct((m, dm), F32)
        tb_rows = None
        scratch = [pltpu.VMEM((bm, bn), F32)] if nk > 1 else []
    in_specs = [pl.BlockSpec((bm, bk), lambda j, i, k: (i, k)),
                pl.BlockSpec((None, bk, bn), lambda j, i, k: (layer, k, j)),
                x_spec, gate_spec]
    args = [lhs, w, x_in, gate]
    rc = min(bm, 256)
    vmem = (2 * bm * bk * 2 + 2 * bk * bn * 2 + 4 * bm * bn * 4 + (2 + 2 * time_batch) * bm * bn * 4
            + 6 * rc * bn * 4 + 4 * MIB)
    out = pl.pallas_call(
        functools.partial(_mm_res_kernel, nk=nk, tb_rows=tb_rows, rc=rc),
        grid=(dm // bn, m // bm, nk),
        in_specs=in_specs,
        out_specs=x_spec,
        out_shape=out_shape,
        scratch_shapes=scratch,
        compiler_params=_params(("parallel", "parallel", "arbitrary"), vmem),
        name="mm_res_tb" if time_batch else "mm_res",
    )(*args)
    return out.reshape(b, t, dm)


def _final_norm_kernel(x_ref, g_ref, o_ref):
    x = x_ref[...]
    ms = jnp.mean(x * x, axis=-1, keepdims=True)
    o_ref[...] = (x * lax.rsqrt(ms + EPS)) * g_ref[...]


def _final_norm(x, g, bm=512):
    b, t, dm = x.shape
    m = b * t
    out = pl.pallas_call(
        _final_norm_kernel,
        grid=(m // bm,),
        in_specs=[pl.BlockSpec((bm, dm), lambda i: (i, 0)),
                  pl.BlockSpec((1, dm), lambda i: (0, 0))],
        out_specs=pl.BlockSpec((bm, dm), lambda i: (i, 0)),
        out_shape=jax.ShapeDtypeStruct((m, dm), F32),
        compiler_params=_params(("parallel",), 8 * bm * dm * 4 + 4 * MIB),
        name="final_norm",
    )(x.reshape(m, dm), g)
    return out.reshape(b, t, dm)


def _ffn_up_kernel(x_ref, xh_ref, g_ref, sc_ref, sh_ref, wg_ref, wu_ref, cwg_ref, cwu_ref,
                   cbg_ref, cbu_ref, o_ref, lhs_ref, inv_ref, *, bm, rc, kc):
    i, j = pl.program_id(1), pl.program_id(2)
    dm = x_ref.shape[-1]
    bn = o_ref.shape[-1]
    up_cols = [pl.ds(s * V7X_MXU_COLS, V7X_MXU_COLS) for s in range(bn // V7X_MXU_COLS)]

    def conv(a, cw_ref, cb_ref, cols):
        a1 = pltpu.roll(a, 1, axis=0)
        a2 = pltpu.roll(a, 2, axis=0)
        cw = cw_ref[:, cols]
        return (cb_ref[:, cols] + cw[0:1] * a2[CONV_HALO:] + cw[1:2] * a1[CONV_HALO:]
                + cw[2:3] * a[CONV_HALO:])

    def glu(act, a_up, s):
        up = conv(a_up, cwu_ref, cbu_ref, up_cols[s])
        o_ref[0, :, up_cols[s]] = (act[:, s * V7X_MXU_COLS:(s + 1) * V7X_MXU_COLS] * up).astype(BF16)

    @pl.when(j == 0)
    def _():
        g, sc, sh = g_ref[...], sc_ref[0], sh_ref[0]

        def inv_rms(x):
            return jnp.broadcast_to(lax.rsqrt(jnp.mean(x * x, axis=-1, keepdims=True) + EPS),
                                    (x.shape[0], V7X_LANES))

        inv_ref[pl.ds(0, CONV_HALO), :] = inv_rms(xh_ref[0])
        for c in range(bm // rc):
            inv_ref[pl.ds(CONV_HALO + c * rc, rc), :] = inv_rms(x_ref[0, pl.ds(c * rc, rc), :])

        a_g, a_u = None, [None] * len(up_cols)
        assert sum(kc) == dm
        for k0, kw in zip((sum(kc[:n]) for n in range(len(kc))), kc):
            for lc in range(k0 // V7X_LANES, (k0 + kw) // V7X_LANES):
                lanes = pl.ds(lc * V7X_LANES, V7X_LANES)
                lo, hi = lc * V7X_LANES, (lc + 1) * V7X_LANES

                def norm_mod(x, inv):
                    return ((x * inv) * g[:, lo:hi]) * (1.0 + sc[:, lo:hi]) + sh[:, lo:hi]

                halo = norm_mod(xh_ref[0, :, lanes], inv_ref[pl.ds(0, CONV_HALO), :])
                lhs_ref[pl.ds(0, CONV_HALO), lanes] = jnp.where(i == 0, 0.0, halo).astype(BF16)
                for c in range(bm // rc):
                    rows = pl.ds(CONV_HALO + c * rc, rc)
                    lhs_ref[rows, lanes] = norm_mod(x_ref[0, pl.ds(c * rc, rc), lanes],
                                                    inv_ref[rows, :]).astype(BF16)
            klanes = pl.ds(k0, kw)
            lk = lhs_ref[:, klanes]
            part = jnp.dot(lk, wg_ref[klanes, :], preferred_element_type=F32)
            a_g = part if a_g is None else a_g + part
            for s, cols in enumerate(up_cols):
                part = jnp.dot(lk, wu_ref[klanes, cols], preferred_element_type=F32)
                a_u[s] = part if a_u[s] is None else a_u[s] + part
        gate = conv(a_g, cwg_ref, cbg_ref, pl.ds(0, bn))
        act = gate * _sigmoid(gate)
        for s in range(len(up_cols)):
            glu(act, a_u[s], s)

    @pl.when(j > 0)
    def _():
        lhs = lhs_ref[...]
        gate = conv(jnp.dot(lhs, wg_ref[...], preferred_element_type=F32), cwg_ref, cbg_ref,
                    pl.ds(0, bn))
        act = gate * _sigmoid(gate)
        for s, cols in enumerate(up_cols):
            glu(act, jnp.dot(lhs, wu_ref[:, cols], preferred_element_type=F32), s)


def _ffn_up(x, g, sc, sh, w_up, conv_w, conv_b, bm=1024, bn=512):
    b, t, dm = x.shape
    w_up, layer = w_up
    f = w_up.shape[-1] // 2
    assert t % bm == 0 and f % bn == 0 and bm % CONV_HALO == 0
    nj = f // bn
    rc = 256
    vmem = (2 * bm * dm * 4 + (bm + CONV_HALO) * dm * 2 + 4 * dm * bn * 2 + 2 * bm * bn * 2
            + 8 * (bm + CONV_HALO) * bn * 4 + 4 * rc * dm * 4 + 4 * MIB)
    halo_blocks = bm // CONV_HALO
    return pl.pallas_call(
        functools.partial(_ffn_up_kernel, bm=bm, rc=rc, kc=LHS_K_CHUNKS),
        grid=(b, t // bm, nj),
        in_specs=[pl.BlockSpec((1, bm, dm), lambda bi, i, j: (bi, i, 0)),
                  pl.BlockSpec((1, CONV_HALO, dm),
                               lambda bi, i, j: (bi, jnp.maximum(i * halo_blocks - 1, 0), 0)),
                  pl.BlockSpec((1, dm), lambda bi, i, j: (0, 0)),
                  pl.BlockSpec((1, 1, dm), lambda bi, i, j: (bi, 0, 0)),
                  pl.BlockSpec((1, 1, dm), lambda bi, i, j: (bi, 0, 0)),
                  pl.BlockSpec((None, dm, bn), lambda bi, i, j: (layer, 0, j)),
                  pl.BlockSpec((None, dm, bn), lambda bi, i, j: (layer, 0, nj + j)),
                  pl.BlockSpec((CONV_W, bn), lambda bi, i, j: (0, j)),
                  pl.BlockSpec((CONV_W, bn), lambda bi, i, j: (0, nj + j)),
                  pl.BlockSpec((1, bn), lambda bi, i, j: (0, j)),
                  pl.BlockSpec((1, bn), lambda bi, i, j: (0, nj + j))],
        out_specs=pl.BlockSpec((1, bm, bn), lambda bi, i, j: (bi, i, j)),
        out_shape=jax.ShapeDtypeStruct((b, t, f), BF16),
        scratch_shapes=[pltpu.VMEM((bm + CONV_HALO, dm), BF16),
                        pltpu.VMEM((bm + CONV_HALO, V7X_LANES), F32)],
        compiler_params=_params(("parallel", "parallel", "arbitrary"), vmem),
        name="ffn_up",
    )(x, x, g, sc, sh, w_up, w_up, conv_w, conv_w, conv_b, conv_b)


def _ssm_disc_kernel(ls_ref, ar_ref, ai_ref, br_ref, bi_ref, lr_ref, li_ref, bbr_ref, bbi_ref):
    step = jnp.exp(ls_ref[...])
    lr, li = ar_ref[...], ai_ref[...]
    mag = jnp.exp(lr * step)
    ang = li * step
    lb_re = mag * jnp.cos(ang)
    lb_im = mag * jnp.sin(ang)
    nr, ni = lb_re - 1.0, lb_im
    den = lr * lr + li * li
    coef_re = (nr * lr + ni * li) / den
    coef_im = (ni * lr - nr * li) / den
    br, bi = br_ref[...], bi_ref[...]
    lr_ref[...] = lb_re
    li_ref[...] = lb_im
    bbr_ref[...] = coef_re * br - coef_im * bi
    bbi_ref[...] = coef_re * bi + coef_im * br


def _ssm_discretize(log_step, a_re, a_im, b_re, b_im, bg=SSM_CHUNK_GROUPS):
    g, p, c = b_re.shape
    col = jax.ShapeDtypeStruct((g, p, 1), F32)
    full = jax.ShapeDtypeStruct((g, p, c), F32)
    col_spec = pl.BlockSpec((bg, p, 1), lambda i: (i, 0, 0))
    full_spec = pl.BlockSpec((bg, p, c), lambda i: (i, 0, 0))
    return pl.pallas_call(
        _ssm_disc_kernel,
        grid=(g // bg,),
        in_specs=[pl.BlockSpec((bg, 1, 1), lambda i: (i, 0, 0)), col_spec, col_spec,
                  full_spec, full_spec],
        out_specs=(col_spec, col_spec, full_spec, full_spec),
        out_shape=(col, col, full, full),
        compiler_params=_params(("parallel",), 32 * MIB),
        name="ssm_discretize",
    )(log_step.reshape(g, 1, 1), a_re.reshape(g, p, 1), a_im.reshape(g, p, 1), b_re, b_im)


def _ssm_in_kernel(x_ref, g_ref, sc_ref, sh_ref, w_ref, o_ref, hs_ref, hs2_ref, lhs_ref, *, nb, bt):
    g = g_ref[...]
    nlc = x_ref.shape[-1] // V7X_LANES
    for bi in range(nb):
        h = _norm_mod(x_ref[bi], g, sc_ref[bi], sh_ref[bi])
        for lc in range(nlc):
            _scatter_class(hs2_ref, lc, nb * bt, nb, bi, h[:, lc * V7X_LANES:(lc + 1) * V7X_LANES])
    for lc in range(nlc):
        _scatter_finish(hs_ref, hs2_ref, lc, nb * bt)
        lhs_ref[:, pl.ds(lc * V7X_LANES, V7X_LANES)] = hs_ref[lc].astype(BF16)
    o_ref[...] = jnp.dot(lhs_ref[...], w_ref[...], preferred_element_type=F32)


def _ssm_in(x, g, sc, sh, w, bt=32):
    b, t, dm = x.shape
    w, layer = w
    n = w.shape[-1]
    bm = bt * b
    vmem = (2 * bm * dm * 4 + 2 * bm * dm * 4 + bm * dm * 2 + 2 * dm * n * 2 + 2 * bm * n * 4
            + bm * n * 4 + 4 * MIB)
    return pl.pallas_call(
        functools.partial(_ssm_in_kernel, nb=b, bt=bt),
        grid=(t // bt,),
        in_specs=[pl.BlockSpec((b, bt, dm), lambda i: (0, i, 0)),
                  pl.BlockSpec((1, dm), lambda i: (0, 0)),
                  pl.BlockSpec((b, 1, dm), lambda i: (0, 0, 0)),
                  pl.BlockSpec((b, 1, dm), lambda i: (0, 0, 0)),
                  pl.BlockSpec((None, dm, n), lambda i: (layer, 0, 0))],
        out_specs=pl.BlockSpec((bm, n), lambda i: (i, 0)),
        out_shape=jax.ShapeDtypeStruct((t * b, n), F32),
        scratch_shapes=[pltpu.VMEM((dm // V7X_LANES, bm, V7X_LANES), F32),
                        pltpu.VMEM((dm // V7X_LANES, bm, V7X_LANES), F32),
                        pltpu.VMEM((bm, dm), BF16)],
        compiler_params=_params(("parallel",), vmem),
        name="ssm_in_proj",
    )(x, g, sc, sh, w)


def _gelu_tanh(y):
    return 0.5 * y * (1.0 + jnp.tanh(math.sqrt(2.0 / math.pi) * (y + 0.044715 * (y * y * y))))


def _ssm_core_kernel(u_ref, bmat_ref, lr_ref, li_ref, cre_ref, cim_ref, dsk_ref, o_ref,
                     st_ref, buf_ref, *, nb, rb):
    s = SSM_CHUNK_STATES
    hw = SSM_SCAN_LANES
    nhalf = s // hw

    @pl.when(pl.program_id(1) == 0)
    def _():
        st_ref[...] = jnp.zeros_like(st_ref)

    ub = u_ref[...].astype(BF16)

    def state_cols(half):
        return pl.ds(half * hw, hw), pl.ds(s + half * hw, hw)

    def b_proj(half):
        for cols in state_cols(half):
            buf_ref[:, cols] = jnp.dot(ub, bmat_ref[0, :, cols], preferred_element_type=F32)

    def scan(half):
        cr, ci = state_cols(half)
        lr = jnp.broadcast_to(lr_ref[0, :, cr], (nb, hw))
        li = jnp.broadcast_to(li_ref[0, :, cr], (nb, hw))
        xr, xi = st_ref[:, cr], st_ref[:, ci]
        for tt in range(rb // nb):
            rows = pl.ds(tt * nb, nb)
            xr, xi = (lr * xr - li * xi + buf_ref[rows, cr], lr * xi + li * xr + buf_ref[rows, ci])
            buf_ref[rows, cr] = xr
            buf_ref[rows, ci] = xi
        st_ref[:, cr] = xr
        st_ref[:, ci] = xi

    def c_proj(half):
        cr, ci = state_cols(half)
        srows = pl.ds(half * hw, hw)
        return (jnp.dot(buf_ref[:, cr].astype(BF16), cre_ref[0, srows, :], preferred_element_type=F32)
                - jnp.dot(buf_ref[:, ci].astype(BF16), cim_ref[0, srows, :],
                          preferred_element_type=F32))

    b_proj(0)
    y = None
    for half in range(nhalf):
        if half + 1 < nhalf:
            b_proj(half + 1)
        scan(half)
        part = c_proj(half)
        y = part if y is None else y + part

    y = y + dsk_ref[0] * u_ref[...]
    o_ref[...] = _gelu_tanh(y).astype(BF16)


def _ssm_core(u, bmat, lam_re, lam_im, cre, cim, d_skip, nb, rb=2048):
    m, dm = u.shape
    nchunk = dm // SSM_CHUNK
    s = SSM_CHUNK_STATES
    vmem = (2 * rb * SSM_CHUNK * 4 + 2 * SSM_CHUNK * 2 * s * 2 + 4 * s * SSM_CHUNK * 2
            + 2 * rb * SSM_CHUNK * 2 + rb * 2 * s * 4 + nb * 2 * s * 4
            + 3 * rb * SSM_SCAN_LANES * 4 + 4 * MIB)
    return pl.pallas_call(
        functools.partial(_ssm_core_kernel, nb=nb, rb=rb),
        grid=(nchunk, m // rb),
        in_specs=[pl.BlockSpec((rb, SSM_CHUNK), lambda c, i: (i, c)),
                  pl.BlockSpec((1, SSM_CHUNK, 2 * s), lambda c, i: (c, 0, 0)),
                  pl.BlockSpec((1, 1, s), lambda c, i: (c, 0, 0)),
                  pl.BlockSpec((1, 1, s), lambda c, i: (c, 0, 0)),
                  pl.BlockSpec((1, s, SSM_CHUNK), lambda c, i: (c, 0, 0)),
                  pl.BlockSpec((1, s, SSM_CHUNK), lambda c, i: (c, 0, 0)),
                  pl.BlockSpec((1, 1, SSM_CHUNK), lambda c, i: (c, 0, 0))],
        out_specs=pl.BlockSpec((rb, SSM_CHUNK), lambda c, i: (i, c)),
        out_shape=jax.ShapeDtypeStruct((m, dm), BF16),
        scratch_shapes=[pltpu.VMEM((nb, 2 * s), F32), pltpu.VMEM((rb, 2 * s), F32)],
        compiler_params=_params(("parallel", "arbitrary"), vmem),
        name="ssm_core",
    )(u, bmat, lam_re, lam_im, cre, cim, d_skip)


def _ssm_gate_kernel(g_ref, w_ref, o_ref):
    g = g_ref[...]
    z = jnp.dot(g, w_ref[...], preferred_element_type=F32)
    o_ref[...] = (g.astype(F32) * _sigmoid(z)).astype(BF16)


def _ssm_gate(g, w, bm=512):
    m, dm = g.shape
    w, layer = w
    vmem = 4 * bm * dm * 2 + 2 * dm * dm * 2 + 4 * bm * dm * 4 + 4 * MIB
    return pl.pallas_call(
        _ssm_gate_kernel,
        grid=(m // bm,),
        in_specs=[pl.BlockSpec((bm, dm), lambda i: (i, 0)),
                  pl.BlockSpec((None, dm, dm), lambda i: (layer, 0, 0))],
        out_specs=pl.BlockSpec((bm, dm), lambda i: (i, 0)),
        out_shape=jax.ShapeDtypeStruct((m, dm), BF16),
        compiler_params=_params(("parallel",), vmem),
        name="ssm_glu",
    )(g, w)


def _block_diag_chunks(a):
    g, r, s = a.shape
    n = SSM_CHUNK_GROUPS
    eye = jnp.eye(n, dtype=a.dtype)
    out = a.reshape(g // n, n, r, 1, s) * eye.reshape(1, n, 1, n, 1)
    return out.reshape(g // n, n * r, n * s)


def _s5_mixer(x, g, sc, sh, gate, w_in, log_step, a_re, a_im, b_re, b_im, c_re, c_im, d_skip,
              w_gate, w_out):
    b, t, dm = x.shape
    ng = dm // SSM_GROUP
    nchunk = dm // SSM_CHUNK
    s = SSM_CHUNK_STATES
    lam_re, lam_im, bb_re, bb_im = _ssm_discretize(log_step, a_re, a_im, b_re, b_im)
    bmat = jnp.concatenate([_block_diag_chunks(bb_re.transpose(0, 2, 1)),
                            _block_diag_chunks(bb_im.transpose(0, 2, 1))], axis=-1).astype(BF16)
    cre = _block_diag_chunks(c_re.transpose(0, 2, 1)).astype(BF16)
    cim = _block_diag_chunks(c_im.transpose(0, 2, 1)).astype(BF16)
    lam_re = lam_re.reshape(nchunk, 1, s)
    lam_im = lam_im.reshape(nchunk, 1, s)
    u = _ssm_in(x, g, sc, sh, w_in)
    gq = _ssm_core(u, bmat, lam_re, lam_im, cre, cim, d_skip.reshape(nchunk, 1, SSM_CHUNK), nb=b)
    gq = _ssm_gate(gq, w_gate)
    return _mm_res(gq, w_out, x, gate, bm=32 * b, bk=dm, time_batch=True)


def kernel(x, c, ada_w, ada_b, norm1_g, norm2_g, attn_w_in, attn_w_out, ssm_w_in, ssm_log_step, ssm_a_re, ssm_a_im, ssm_b_re, ssm_b_im, ssm_c_re, ssm_c_im, ssm_d, ssm_w_gate, ssm_w_out, ffn_w_up, ffn_conv_w, ffn_conv_b, ffn_w_down, final_norm_g):
    b, t, dm = x.shape
    depth = ada_w.shape[0]
    hd = N_HEADS * HEAD_DIM
    slopes = 2.0 ** (-8.0 * jnp.arange(1, N_HEADS + 1, dtype=F32) / N_HEADS)

    attn_w_in, attn_w_out, ssm_w_in, ssm_w_gate, ssm_w_out, ffn_w_up, ffn_w_down = (
        w.astype(BF16) for w in (attn_w_in, attn_w_out, ssm_w_in, ssm_w_gate, ssm_w_out,
                                 ffn_w_up, ffn_w_down))

    mod = _ada(c, ada_w, ada_b).reshape(depth, b, 6, 1, dm)
    for i in range(depth):
        j = i // 2
        sh1, sc1, g1, sh2, sc2, g2 = (mod[i, :, q] for q in range(6))
        n1 = norm1_g[i].reshape(1, dm)
        if i % 2 == 0:
            groups = []
            for gi, (_, d) in enumerate(DILATED_GROUPS):
                if d == 1:
                    qkv, v_shared = _attn_proj(x, n1, sc1, sh1, (attn_w_in, j), gi, d)
                    groups.append(((qkv, 0), (qkv, N_HEADS), (qkv, 2 * N_HEADS)))
                else:
                    qk = _attn_proj(x, n1, sc1, sh1, (attn_w_in, j), gi, d)
                    groups.append(((qk, 0), (qk, N_HEADS), (v_shared[gi - 1], 0)))
            o = _attention(groups, slopes)
            x = _mm_res(o.reshape(b * t, hd), (attn_w_out, j), x, g1, bm=512, bk=hd)
        else:
            x = _s5_mixer(x, n1, sc1, sh1, g1, (ssm_w_in, j), ssm_log_step[j],
                          ssm_a_re[j], ssm_a_im[j], ssm_b_re[j], ssm_b_im[j], ssm_c_re[j],
                          ssm_c_im[j], ssm_d[j], (ssm_w_gate, j), (ssm_w_out, j))
        a = _ffn_up(x, norm2_g[i].reshape(1, dm), sc2, sh2, (ffn_w_up, i),
                    ffn_conv_w[i], ffn_conv_b[i].reshape(1, 2 * D_FF))
        x = _mm_res(a.reshape(b * t, D_FF), (ffn_w_down, i), x, g2, bm=512, bk=D_FF, bn=dm // 2)
    return _final_norm(x, final_norm_g.reshape(1, dm))
```

```python
import functools
import math

import jax
import jax.numpy as jnp
from jax import lax
from jax.experimental import pallas as pl
from jax.experimental.pallas import tpu as pltpu

F32 = jnp.float32
BF16 = jnp.bfloat16

D_MODEL = 2048
N_HEADS = 16
HEAD_DIM = 128
DILATED_GROUPS = ((128, 1), (512, 4), (2048, 16))
ATTN_BLOCK = 128
SSM_GROUP = 16
SSM_STATE = 64
D_FF = 5632
CONV_W = 3
EPS = 1e-6

V7X_LANES = 128
V7X_BF16_SUBLANES = 16
V7X_MXU_COLS = 256
SAFE_STRIDE = 4
V7X_SCOPED_VMEM_CAP_BYTES = 60000 * 1024
MIB = 1024 * 1024

SSM_CHUNK = 256
SSM_CHUNK_GROUPS = SSM_CHUNK // SSM_GROUP
SSM_CHUNK_STATES = SSM_CHUNK_GROUPS * SSM_STATE
SSM_SCAN_LANES = 512
LHS_K_CHUNKS = (512, 1536)
CONV_HALO = 16


def _params(semantics, vmem_bytes):
    return pltpu.CompilerParams(
        dimension_semantics=semantics,
        vmem_limit_bytes=int(min(vmem_bytes, V7X_SCOPED_VMEM_CAP_BYTES)))


def _sigmoid(x):
    return 1.0 / (1.0 + jnp.exp(-x))


def _norm_mod(x, g, sc, sh):
    ms = jnp.mean(x * x, axis=-1, keepdims=True)
    y = x * lax.rsqrt(ms + EPS)
    return (y * g) * (1.0 + sc) + sh


def _ada_kernel(c_ref, w_ref, b_ref, o_ref):
    c = c_ref[...]
    cs = (c * _sigmoid(c)).astype(BF16)
    o_ref[0] = jnp.dot(cs, w_ref[0].astype(BF16), preferred_element_type=F32) + b_ref[0]


def _ada(c, ada_w, ada_b, bn=1024):
    depth, d, n = ada_w.shape
    b = c.shape[0]
    vmem = 2 * (d * bn * 4) + d * bn * 2 + 4 * b * (d + bn) * 4 + 4 * MIB
    return pl.pallas_call(
        _ada_kernel,
        grid=(depth, n // bn),
        in_specs=[pl.BlockSpec((b, d), lambda l, j: (0, 0)),
                  pl.BlockSpec((1, d, bn), lambda l, j: (l, 0, j)),
                  pl.BlockSpec((1, 1, bn), lambda l, j: (l, 0, j))],
        out_specs=pl.BlockSpec((1, b, bn), lambda l, j: (l, 0, j)),
        out_shape=jax.ShapeDtypeStruct((depth, b, n), F32),
        compiler_params=_params(("parallel", "parallel"), vmem),
        name="ada_mod",
    )(c, ada_w, ada_b.reshape(depth, 1, n))


def _gather_classes(stage_ref, stage2_ref, chunk, nrows, dils, emit):
    quarter = nrows // SAFE_STRIDE
    for r0 in range(SAFE_STRIDE):
        rows = stage_ref[chunk, pl.ds(r0, quarter, stride=SAFE_STRIDE), :]
        if SAFE_STRIDE in dils:
            emit(SAFE_STRIDE, r0, rows)
        if any(d > SAFE_STRIDE for d in dils):
            stage2_ref[chunk, pl.ds(r0 * quarter, quarter), :] = rows
    for d in dils:
        if d > SAFE_STRIDE:
            step = d // SAFE_STRIDE
            assert step <= SAFE_STRIDE
            for r0 in range(SAFE_STRIDE):
                for r1 in range(step):
                    emit(d, SAFE_STRIDE * r1 + r0,
                         stage2_ref[chunk, pl.ds(r0 * quarter + r1, nrows // d, stride=step), :])


def _scatter_class(stage2_ref, chunk, nrows, d, r, rows):
    step = d // SAFE_STRIDE
    assert SAFE_STRIDE < d and step <= SAFE_STRIDE
    r0, r1 = r % SAFE_STRIDE, r // SAFE_STRIDE
    stage2_ref[chunk, pl.ds(r0 * (nrows // SAFE_STRIDE) + r1, nrows // d, stride=step), :] = rows


def _scatter_finish(stage_ref, stage2_ref, chunk, nrows):
    quarter = nrows // SAFE_STRIDE
    for r0 in range(SAFE_STRIDE):
        stage_ref[chunk, pl.ds(r0, quarter, stride=SAFE_STRIDE), :] = (
            stage2_ref[chunk, pl.ds(r0 * quarter, quarter), :])


def _attn_proj_kernel(x_ref, g_ref, sc_ref, sh_ref, w_ref, o_ref, *rest, d, bm, bn, rc, kc, v_dils,
                      v_first_block):
    nv = len(v_dils)
    v_refs, lhs_ref, inv_ref, stage_ref = rest[:nv], rest[nv], rest[nv + 1], rest[nv + 2]
    stage2_ref = rest[nv + 3] if len(rest) > nv + 3 else None
    j = pl.program_id(2)
    rpc = bm // d
    dm = x_ref.shape[-1]
    heads = bn // HEAD_DIM

    def write_out(acc):
        for hh in range(heads):
            for r in range(d):
                o_ref[0, hh, r] = acc[r * rpc:(r + 1) * rpc,
                                      hh * HEAD_DIM:(hh + 1) * HEAD_DIM].astype(BF16)

    @pl.when(j == 0)
    def _():
        g, sc, sh = g_ref[...], sc_ref[0], sh_ref[0]
        for c in range(bm // rc):
            rows = pl.ds(c * rc, rc)
            x = x_ref[0, rows, :]
            inv = lax.rsqrt(jnp.mean(x * x, axis=-1, keepdims=True) + EPS)
            inv_ref[rows, :] = jnp.broadcast_to(inv, (rc, V7X_LANES))
        acc = None
        assert sum(kc) == dm
        for k0, kw in zip((sum(kc[:n]) for n in range(len(kc))), kc):
            for lc in range(k0 // V7X_LANES, (k0 + kw) // V7X_LANES):
                lanes = pl.ds(lc * V7X_LANES, V7X_LANES)
                lo, hi = lc * V7X_LANES, (lc + 1) * V7X_LANES
                for c in range(bm // rc):
                    rows = pl.ds(c * rc, rc)
                    h = (((x_ref[0, rows, lanes] * inv_ref[rows, :]) * g[:, lo:hi])
                         * (1.0 + sc[:, lo:hi]) + sh[:, lo:hi])
                    if d == 1:
                        lhs_ref[rows, lanes] = h.astype(BF16)
                    else:
                        stage_ref[lc, rows, :] = h
                if d > 1:
                    def to_lhs(_, r, rows, lanes=lanes):
                        lhs_ref[pl.ds(r * rpc, rpc), lanes] = rows.astype(BF16)
                    _gather_classes(stage_ref, stage2_ref, lc, bm, (d,), to_lhs)
            klanes = pl.ds(k0, kw)
            part = jnp.dot(lhs_ref[:, klanes], w_ref[klanes, :], preferred_element_type=F32)
            acc = part if acc is None else acc + part
        write_out(acc)

    @pl.when(j > 0)
    def _():
        acc = jnp.dot(lhs_ref[...], w_ref[...], preferred_element_type=F32)
        write_out(acc)
        if v_refs:
            @pl.when(j >= v_first_block)
            def _():
                for hh in range(heads):
                    stage_ref[hh] = acc[:, hh * HEAD_DIM:(hh + 1) * HEAD_DIM]
                for hh in range(heads):
                    def to_v(dd, r, rows, hh=hh):
                        v_refs[v_dils.index(dd)][0, hh, r] = rows.astype(BF16)
                    _gather_classes(stage_ref, stage2_ref, hh, bm, v_dils, to_v)


def _attn_proj(x, g, sc, sh, w, gi, d, bm=1024, bn=1024):
    b, t, dm = x.shape
    w, layer = w
    hd = N_HEADS * HEAD_DIM
    parts = 3 if d == 1 else 2
    n = parts * hd
    per_part = hd // bn
    n_groups = len(DILATED_GROUPS)
    part_start = (gi * per_part, (n_groups + gi) * per_part, 2 * n_groups * per_part)
    v_dils = tuple(dd for _, dd in DILATED_GROUPS if dd > 1) if d == 1 else ()
    v_first_block = 2 * per_part

    def w_block(bi, i, j):
        part = j // per_part
        start = jnp.where(part == 0, part_start[0],
                          jnp.where(part == 1, part_start[1], part_start[2]))
        return (layer, 0, start + j % per_part)

    rpc = bm // d
    rc = 256
    heads = bn // HEAD_DIM
    assert rpc % V7X_BF16_SUBLANES == 0 and bm % rc == 0 and t % bm == 0 and hd % bn == 0
    assert all((bm // dd) % V7X_BF16_SUBLANES == 0 for dd in v_dils)
    stage_chunks = heads if d == 1 else dm // V7X_LANES
    n_stages = 2 if max(v_dils + (d,)) > SAFE_STRIDE else 1
    scratch = ([pltpu.VMEM((bm, dm), BF16), pltpu.VMEM((bm, V7X_LANES), F32)]
               + [pltpu.VMEM((stage_chunks, bm, V7X_LANES), F32)] * n_stages)
    out_specs = [pl.BlockSpec((1, heads, d, rpc, HEAD_DIM), lambda bi, i, j: (bi, j, 0, i, 0))]
    out_shape = [jax.ShapeDtypeStruct((b, n // HEAD_DIM, d, t // d, HEAD_DIM), BF16)]
    for dd in v_dils:
        out_specs.append(pl.BlockSpec(
            (1, heads, dd, bm // dd, HEAD_DIM),
            lambda bi, i, j: (bi, jnp.maximum(j - v_first_block, 0), 0, i, 0)))
        out_shape.append(jax.ShapeDtypeStruct((b, N_HEADS, dd, t // dd, HEAD_DIM), BF16))
    vmem = (2 * bm * dm * 4 + bm * dm * 2 + n_stages * stage_chunks * bm * V7X_LANES * 4
            + 2 * dm * bn * 2
            + 2 * (1 + len(v_dils)) * bm * bn * 2 + bm * bn * 4 + 4 * rc * dm * 4 + 4 * MIB)
    outs = pl.pallas_call(
        functools.partial(_attn_proj_kernel, d=d, bm=bm, bn=bn, rc=rc, kc=LHS_K_CHUNKS, v_dils=v_dils,
                          v_first_block=v_first_block),
        grid=(b, t // bm, n // bn),
        in_specs=[pl.BlockSpec((1, bm, dm), lambda bi, i, j: (bi, i, 0)),
                  pl.BlockSpec((1, dm), lambda bi, i, j: (0, 0)),
                  pl.BlockSpec((1, 1, dm), lambda bi, i, j: (bi, 0, 0)),
                  pl.BlockSpec((1, 1, dm), lambda bi, i, j: (bi, 0, 0)),
                  pl.BlockSpec((None, dm, bn), w_block)],
        out_specs=out_specs,
        out_shape=out_shape,
        scratch_shapes=scratch,
        compiler_params=_params(("parallel", "parallel", "arbitrary"), vmem),
        name=f"attn_proj_d{d}",
    )(x, g, sc, sh, w)
    outs = [o.reshape(b, o.shape[1], t, HEAD_DIM) for o in outs]
    return (outs[0], outs[1:]) if d == 1 else outs[0]


def _attn_kernel(slopes_ref, q1, k1, v1, q4, k4, v4, q16, k16, v16, o_ref, o_scr, l_scr, *, t,
                 stage_batch):
    blk = ATTN_BLOCK
    slope = slopes_ref[pl.program_id(1)]
    scale = HEAD_DIM ** -0.5
    qi = lax.broadcasted_iota(jnp.int32, (blk, 2 * blk), 0)
    kj = lax.broadcasted_iota(jnp.int32, (blk, 2 * blk), 1)
    dist = blk + qi - kj
    in_band = (dist >= 0) & (dist <= blk)

    refs = ((q1, k1, v1), (q4, k4, v4), (q16, k16, v16))
    stage_slot = len(DILATED_GROUPS)
    quarter = t // SAFE_STRIDE
    assert sum(d > SAFE_STRIDE for _, d in DILATED_GROUPS) <= 1
    work = []
    for gi, ((_, d), (q_ref, k_ref, v_ref)) in enumerate(zip(DILATED_GROUPS, refs)):
        cls_len = t // d
        nb = cls_len // blk
        bias_full = jnp.where(in_band, -slope * (dist * d).astype(F32), -jnp.inf)
        bias_first = bias_full[:, blk:]

        for r in range(d):
            base = r * cls_len
            for n in range(nb):
                if d == 1:
                    dst = (gi, pl.ds(n * blk, blk))
                elif d <= SAFE_STRIDE:
                    dst = (gi, pl.ds(n * blk * d + r, blk, stride=d))
                else:
                    step = d // SAFE_STRIDE
                    r0, r1 = r % SAFE_STRIDE, r // SAFE_STRIDE
                    dst = (stage_slot, pl.ds(r0 * quarter + step * n * blk + r1, blk, stride=step))
                q_rows = pl.ds(base + n * blk, blk)
                if n == 0:
                    work.append((q_ref, k_ref, v_ref, q_rows, pl.ds(base, blk), bias_first, dst))
                else:
                    work.append((q_ref, k_ref, v_ref, q_rows,
                                 pl.ds(base + (n - 1) * blk, 2 * blk), bias_full, dst))

    for w0 in range(0, len(work), stage_batch):
        batch = work[w0:w0 + stage_batch]
        ss = [lax.dot_general(q_ref[0, 0, qr, :], k_ref[0, 0, kr, :], (((1,), (1,)), ((), ())),
                              preferred_element_type=F32) * scale + bias
              for (q_ref, k_ref, _, qr, kr, bias, _) in batch]
        ms = [jnp.max(s, axis=-1, keepdims=True) for s in ss]
        es = [jnp.exp(s - m) for s, m in zip(ss, ms)]
        dens = [jnp.sum(e, axis=-1, keepdims=True) for e in es]
        ps = [(e / den).astype(BF16) for e, den in zip(es, dens)]
        outs = [jnp.dot(p, v_ref[0, 0, kr, :], preferred_element_type=F32)
                for p, (_, _, v_ref, _, kr, _, _) in zip(ps, batch)]
        for o, m, den, (_, _, _, _, _, _, (slot, rows)) in zip(outs, ms, dens, batch):
            o_scr[slot, rows, :] = o
            l_scr[slot, rows, :] = jnp.broadcast_to(m + jnp.log(den), (blk, HEAD_DIM))

    for gi, (_, d) in enumerate(DILATED_GROUPS):
        if d > SAFE_STRIDE:
            for scr in (o_scr, l_scr):
                for r0 in range(SAFE_STRIDE):
                    for c in range(quarter // blk):
                        scr[gi, pl.ds(r0 + c * blk * SAFE_STRIDE, blk, stride=SAFE_STRIDE), :] = (
                            scr[stage_slot, pl.ds(r0 * quarter + c * blk, blk), :])

    rc = 256

    def combine(c, carry):
        rows = pl.ds(pl.multiple_of(c * rc, rc), rc)
        l0, l1, l2 = l_scr[0, rows, :], l_scr[1, rows, :], l_scr[2, rows, :]
        mx = jnp.maximum(jnp.maximum(l0, l1), l2)
        w0, w1, w2 = jnp.exp(l0 - mx), jnp.exp(l1 - mx), jnp.exp(l2 - mx)
        o = (w0 * o_scr[0, rows, :] + w1 * o_scr[1, rows, :] + w2 * o_scr[2, rows, :]) \
            / (w0 + w1 + w2)
        o_ref[0, rows, :] = o.astype(BF16)
        return carry

    lax.fori_loop(0, t // rc, combine, 0, unroll=2)


def _attention(groups, slopes, stage_batch=16):
    b, _, t, dh = groups[0][0][0].shape
    h = N_HEADS
    in_specs = [pl.BlockSpec(memory_space=pltpu.SMEM)]
    args = [slopes]
    for group in groups:
        for arr, head0 in group:
            in_specs.append(pl.BlockSpec(
                (1, 1, t, dh), functools.partial(lambda bi, hi, head0: (bi, head0 + hi, 0, 0),
                                                 head0=head0)))
            args.append(arr)
    vmem = 2 * 9 * t * dh * 2 + 2 * t * dh * 2 + 8 * t * dh * 4 + 8 * MIB
    return pl.pallas_call(
        functools.partial(_attn_kernel, t=t, stage_batch=stage_batch),
        grid=(b, h),
        in_specs=in_specs,
        out_specs=pl.BlockSpec((1, t, dh), lambda bi, hi: (bi, 0, hi)),
        out_shape=jax.ShapeDtypeStruct((b, t, h * dh), BF16),
        scratch_shapes=[pltpu.VMEM((len(groups) + 1, t, dh), F32)] * 2,
        compiler_params=_params(("parallel", "parallel"), vmem),
        name="dilated_attn",
    )(*args)


def _mm_res_kernel(lhs_ref, w_ref, x_ref, gate_ref, o_ref, *scratch, nk, tb_rows, rc):
    acc_ref = accs_ref = None
    if tb_rows is not None:
        acc_ref, accs_ref, accs2_ref = scratch
    elif nk > 1:
        acc_ref, = scratch
    k = pl.program_id(2)
    part = jnp.dot(lhs_ref[...], w_ref[...], preferred_element_type=F32)

    def finish(xs, gate, res):
        return xs + gate * res

    if acc_ref is None:
        o_ref[...] = finish(x_ref[...], gate_ref[0], part)
        return

    if nk == 1:
        acc_ref[...] = part
    else:
        @pl.when(k == 0)
        def _():
            acc_ref[...] = part

        @pl.when(k > 0)
        def _():
            acc_ref[...] += part

    @pl.when(k == nk - 1)
    def _():
        if tb_rows is None:
            gate = gate_ref[0]
            bm = acc_ref.shape[0]
            for c in range(bm // rc):
                rows = pl.ds(c * rc, rc)
                o_ref[rows, :] = finish(x_ref[rows, :], gate, acc_ref[rows, :])
        else:
            nb, bt = tb_rows
            nlc = acc_ref.shape[-1] // V7X_LANES
            for lc in range(nlc):
                lanes = pl.ds(lc * V7X_LANES, V7X_LANES)
                accs_ref[lc] = acc_ref[:, lanes]

                def put(_, bi, res, lanes=lanes):
                    o_ref[bi, :, lanes] = finish(x_ref[bi, :, lanes], gate_ref[bi, :, lanes], res)

                _gather_classes(accs_ref, accs2_ref, lc, nb * bt, (nb,), put)


def _mm_res(lhs, w, x, gate, *, bm, bk, bn=None, time_batch=False):
    b, t, dm = x.shape
    w, layer = w
    m, kdim = lhs.shape
    bn = dm if bn is None else bn
    assert m == b * t and kdim % bk == 0 and m % bm == 0 and dm % bn == 0
    nk = kdim // bk
    if time_batch:
        assert bn == dm and b > SAFE_STRIDE
        bt = bm // b
        x_in, x_spec = x, pl.BlockSpec((b, bt, dm), lambda j, i, k: (0, i, 0))
        gate_spec = pl.BlockSpec((b, 1, dm), lambda j, i, k: (0, 0, 0))
        out_shape = jax.ShapeDtypeStruct((b, t, dm), F32)
        tb_rows = (b, bt)
        scratch = ([pltpu.VMEM((bm, dm), F32)]
                   + [pltpu.VMEM((dm // V7X_LANES, bm, V7X_LANES), F32)] * 2)
    else:
        assert t % bm == 0
        x_in, x_spec = x.reshape(m, dm), pl.BlockSpec((bm, bn), lambda j, i, k: (i, j))
        gate_spec = pl.BlockSpec((1, 1, bn), lambda j, i, k: (i * bm // t, 0, j))
        out_shape = jax.ShapeDtypeStruct((m, dm), F32)
        tb_rows = None
        scratch = [pltpu.VMEM((bm, bn), F32)] if nk > 1 else []
    in_specs = [pl.BlockSpec((bm, bk), lambda j, i, k: (i, k)),
                pl.BlockSpec((None, bk, bn), lambda j, i, k: (layer, k, j)),
                x_spec, gate_spec]
    args = [lhs, w, x_in, gate]
    rc = min(bm, 256)
    vmem = (2 * bm * bk * 2 + 2 * bk * bn * 2 + 4 * bm * bn * 4 + (2 + 2 * time_batch) * bm * bn * 4
            + 6 * rc * bn * 4 + 4 * MIB)
    out = pl.pallas_call(
        functools.partial(_mm_res_kernel, nk=nk, tb_rows=tb_rows, rc=rc),
        grid=(dm // bn, m // bm, nk),
        in_specs=in_specs,
        out_specs=x_spec,
        out_shape=out_shape,
        scratch_shapes=scratch,
        compiler_params=_params(("parallel", "parallel", "arbitrary"), vmem),
        name="mm_res_tb" if time_batch else "mm_res",
    )(*args)
    return out.reshape(b, t, dm)


def _final_norm_kernel(x_ref, g_ref, o_ref):
    x = x_ref[...]
    ms = jnp.mean(x * x, axis=-1, keepdims=True)
    o_ref[...] = (x * lax.rsqrt(ms + EPS)) * g_ref[...]


def _final_norm(x, g, bm=1024):
    b, t, dm = x.shape
    m = b * t
    out = pl.pallas_call(
        _final_norm_kernel,
        grid=(m // bm,),
        in_specs=[pl.BlockSpec((bm, dm), lambda i: (i, 0)),
                  pl.BlockSpec((1, dm), lambda i: (0, 0))],
        out_specs=pl.BlockSpec((bm, dm), lambda i: (i, 0)),
        out_shape=jax.ShapeDtypeStruct((m, dm), F32),
        compiler_params=_params(("parallel",), 8 * bm * dm * 4 + 4 * MIB),
        name="final_norm",
    )(x.reshape(m, dm), g)
    return out.reshape(b, t, dm)


def _ffn_up_kernel(x_ref, xh_ref, g_ref, sc_ref, sh_ref, wg_ref, wu_ref, cwg_ref, cwu_ref,
                   cbg_ref, cbu_ref, o_ref, lhs_ref, inv_ref, *, bm, rc, kc):
    i, j = pl.program_id(1), pl.program_id(2)
    dm = x_ref.shape[-1]
    bn = o_ref.shape[-1]
    up_cols = [pl.ds(s * V7X_MXU_COLS, V7X_MXU_COLS) for s in range(bn // V7X_MXU_COLS)]

    def conv(a, cw_ref, cb_ref, cols):
        cw = cw_ref[:, cols]
        taps = pltpu.roll(cw[0:1] * a, 1, axis=0) + cw[1:2] * a
        taps = pltpu.roll(taps, 1, axis=0) + cw[2:3] * a
        return cb_ref[:, cols] + taps[CONV_HALO:]

    def glu(act, a_up, s):
        up = conv(a_up, cwu_ref, cbu_ref, up_cols[s])
        o_ref[0, :, up_cols[s]] = (act[:, s * V7X_MXU_COLS:(s + 1) * V7X_MXU_COLS] * up).astype(BF16)

    @pl.when(j == 0)
    def _():
        g, sc, sh = g_ref[...], sc_ref[0], sh_ref[0]

        def inv_rms(x):
            return jnp.broadcast_to(lax.rsqrt(jnp.mean(x * x, axis=-1, keepdims=True) + EPS),
                                    (x.shape[0], V7X_LANES))

        inv_ref[pl.ds(0, CONV_HALO), :] = inv_rms(xh_ref[0])
        for c in range(bm // rc):
            inv_ref[pl.ds(CONV_HALO + c * rc, rc), :] = inv_rms(x_ref[0, pl.ds(c * rc, rc), :])

        a_g, a_u = None, [None] * len(up_cols)
        assert sum(kc) == dm
        for k0, kw in zip((sum(kc[:n]) for n in range(len(kc))), kc):
            for lc in range(k0 // V7X_LANES, (k0 + kw) // V7X_LANES):
                lanes = pl.ds(lc * V7X_LANES, V7X_LANES)
                lo, hi = lc * V7X_LANES, (lc + 1) * V7X_LANES

                def norm_mod(x, inv):
                    return ((x * inv) * g[:, lo:hi]) * (1.0 + sc[:, lo:hi]) + sh[:, lo:hi]

                halo = norm_mod(xh_ref[0, :, lanes], inv_ref[pl.ds(0, CONV_HALO), :])
                lhs_ref[pl.ds(0, CONV_HALO), lanes] = jnp.where(i == 0, 0.0, halo).astype(BF16)
                for c in range(bm // rc):
                    rows = pl.ds(CONV_HALO + c * rc, rc)
                    lhs_ref[rows, lanes] = norm_mod(x_ref[0, pl.ds(c * rc, rc), lanes],
                                                    inv_ref[rows, :]).astype(BF16)
            klanes = pl.ds(k0, kw)
            lk = lhs_ref[:, klanes]
            part = jnp.dot(lk, wg_ref[klanes, :], preferred_element_type=F32)
            a_g = part if a_g is None else a_g + part
            for s, cols in enumerate(up_cols):
                part = jnp.dot(lk, wu_ref[klanes, cols], preferred_element_type=F32)
                a_u[s] = part if a_u[s] is None else a_u[s] + part
        gate = conv(a_g, cwg_ref, cbg_ref, pl.ds(0, bn))
        act = gate * _sigmoid(gate)
        for s in range(len(up_cols)):
            glu(act, a_u[s], s)

    @pl.when(j > 0)
    def _():
        lhs = lhs_ref[...]
        gate = conv(jnp.dot(lhs, wg_ref[...], preferred_element_type=F32), cwg_ref, cbg_ref,
                    pl.ds(0, bn))
        act = gate * _sigmoid(gate)
        for s, cols in enumerate(up_cols):
            glu(act, jnp.dot(lhs, wu_ref[:, cols], preferred_element_type=F32), s)


def _ffn_up(x, g, sc, sh, w_up, conv_w, conv_b, bm=1024, bn=512):
    b, t, dm = x.shape
    w_up, layer = w_up
    f = w_up.shape[-1] // 2
    assert t % bm == 0 and f % bn == 0 and bm % CONV_HALO == 0
    nj = f // bn
    rc = 256
    vmem = (2 * bm * dm * 4 + (bm + CONV_HALO) * dm * 2 + 4 * dm * bn * 2 + 2 * bm * bn * 2
            + 8 * (bm + CONV_HALO) * bn * 4 + 4 * rc * dm * 4 + 4 * MIB)
    halo_blocks = bm // CONV_HALO
    return pl.pallas_call(
        functools.partial(_ffn_up_kernel, bm=bm, rc=rc, kc=LHS_K_CHUNKS),
        grid=(b, t // bm, nj),
        in_specs=[pl.BlockSpec((1, bm, dm), lambda bi, i, j: (bi, i, 0)),
                  pl.BlockSpec((1, CONV_HALO, dm),
                               lambda bi, i, j: (bi, jnp.maximum(i * halo_blocks - 1, 0), 0)),
                  pl.BlockSpec((1, dm), lambda bi, i, j: (0, 0)),
                  pl.BlockSpec((1, 1, dm), lambda bi, i, j: (bi, 0, 0)),
                  pl.BlockSpec((1, 1, dm), lambda bi, i, j: (bi, 0, 0)),
                  pl.BlockSpec((None, dm, bn), lambda bi, i, j: (layer, 0, j)),
                  pl.BlockSpec((None, dm, bn), lambda bi, i, j: (layer, 0, nj + j)),
                  pl.BlockSpec((CONV_W, bn), lambda bi, i, j: (0, j)),
                  pl.BlockSpec((CONV_W, bn), lambda bi, i, j: (0, nj + j)),
                  pl.BlockSpec((1, bn), lambda bi, i, j: (0, j)),
                  pl.BlockSpec((1, bn), lambda bi, i, j: (0, nj + j))],
        out_specs=pl.BlockSpec((1, bm, bn), lambda bi, i, j: (bi, i, j)),
        out_shape=jax.ShapeDtypeStruct((b, t, f), BF16),
        scratch_shapes=[pltpu.VMEM((bm + CONV_HALO, dm), BF16),
                        pltpu.VMEM((bm + CONV_HALO, V7X_LANES), F32)],
        compiler_params=_params(("parallel", "parallel", "arbitrary"), vmem),
        name="ffn_up",
    )(x, x, g, sc, sh, w_up, w_up, conv_w, conv_w, conv_b, conv_b)


def _ssm_disc_kernel(ls_ref, ar_ref, ai_ref, br_ref, bi_ref, lr_ref, li_ref, bbr_ref, bbi_ref):
    step = jnp.exp(ls_ref[...])
    lr, li = ar_ref[...], ai_ref[...]
    mag = jnp.exp(lr * step)
    ang = li * step
    lb_re = mag * jnp.cos(ang)
    lb_im = mag * jnp.sin(ang)
    nr, ni = lb_re - 1.0, lb_im
    den = lr * lr + li * li
    coef_re = (nr * lr + ni * li) / den
    coef_im = (ni * lr - nr * li) / den
    br, bi = br_ref[...], bi_ref[...]
    lr_ref[...] = lb_re
    li_ref[...] = lb_im
    bbr_ref[...] = coef_re * br - coef_im * bi
    bbi_ref[...] = coef_re * bi + coef_im * br


def _ssm_discretize(log_step, a_re, a_im, b_re, b_im, bg=SSM_CHUNK_GROUPS):
    g, p, c = b_re.shape
    col = jax.ShapeDtypeStruct((g, p, 1), F32)
    full = jax.ShapeDtypeStruct((g, p, c), F32)
    col_spec = pl.BlockSpec((bg, p, 1), lambda i: (i, 0, 0))
    full_spec = pl.BlockSpec((bg, p, c), lambda i: (i, 0, 0))
    return pl.pallas_call(
        _ssm_disc_kernel,
        grid=(g // bg,),
        in_specs=[pl.BlockSpec((bg, 1, 1), lambda i: (i, 0, 0)), col_spec, col_spec,
                  full_spec, full_spec],
        out_specs=(col_spec, col_spec, full_spec, full_spec),
        out_shape=(col, col, full, full),
        compiler_params=_params(("parallel",), 32 * MIB),
        name="ssm_discretize",
    )(log_step.reshape(g, 1, 1), a_re.reshape(g, p, 1), a_im.reshape(g, p, 1), b_re, b_im)


def _ssm_in_kernel(x_ref, g_ref, sc_ref, sh_ref, w_ref, o_ref, hs_ref, hs2_ref, lhs_ref, *, nb, bt):
    g = g_ref[...]
    nlc = x_ref.shape[-1] // V7X_LANES
    for bi in range(nb):
        h = _norm_mod(x_ref[bi], g, sc_ref[bi], sh_ref[bi])
        for lc in range(nlc):
            _scatter_class(hs2_ref, lc, nb * bt, nb, bi, h[:, lc * V7X_LANES:(lc + 1) * V7X_LANES])
    for lc in range(nlc):
        _scatter_finish(hs_ref, hs2_ref, lc, nb * bt)
        lhs_ref[:, pl.ds(lc * V7X_LANES, V7X_LANES)] = hs_ref[lc].astype(BF16)
    o_ref[...] = jnp.dot(lhs_ref[...], w_ref[...], preferred_element_type=F32)


def _ssm_in(x, g, sc, sh, w, bt=32):
    b, t, dm = x.shape
    w, layer = w
    n = w.shape[-1]
    bm = bt * b
    vmem = (2 * bm * dm * 4 + 2 * bm * dm * 4 + bm * dm * 2 + 2 * dm * n * 2 + 2 * bm * n * 4
            + bm * n * 4 + 4 * MIB)
    return pl.pallas_call(
        functools.partial(_ssm_in_kernel, nb=b, bt=bt),
        grid=(t // bt,),
        in_specs=[pl.BlockSpec((b, bt, dm), lambda i: (0, i, 0)),
                  pl.BlockSpec((1, dm), lambda i: (0, 0)),
                  pl.BlockSpec((b, 1, dm), lambda i: (0, 0, 0)),
                  pl.BlockSpec((b, 1, dm), lambda i: (0, 0, 0)),
                  pl.BlockSpec((None, dm, n), lambda i: (layer, 0, 0))],
        out_specs=pl.BlockSpec((bm, n), lambda i: (i, 0)),
        out_shape=jax.ShapeDtypeStruct((t * b, n), F32),
        scratch_shapes=[pltpu.VMEM((dm // V7X_LANES, bm, V7X_LANES), F32),
                        pltpu.VMEM((dm // V7X_LANES, bm, V7X_LANES), F32),
                        pltpu.VMEM((bm, dm), BF16)],
        compiler_params=_params(("parallel",), vmem),
        name="ssm_in_proj",
    )(x, g, sc, sh, w)


def _gelu_tanh(y):
    return 0.5 * y * (1.0 + jnp.tanh(math.sqrt(2.0 / math.pi) * (y + 0.044715 * (y * y * y))))


def _ssm_core_kernel(u_ref, bmat_ref, lr_ref, li_ref, cre_ref, cim_ref, dsk_ref, o_ref,
                     st_ref, buf_ref, *, nb, rb):
    s = SSM_CHUNK_STATES
    hw = SSM_SCAN_LANES
    nhalf = s // hw

    @pl.when(pl.program_id(1) == 0)
    def _():
        st_ref[...] = jnp.zeros_like(st_ref)

    ub = u_ref[...].astype(BF16)

    def state_cols(half):
        return pl.ds(half * hw, hw), pl.ds(s + half * hw, hw)

    def b_proj(half):
        for cols in state_cols(half):
            buf_ref[:, cols] = jnp.dot(ub, bmat_ref[0, :, cols], preferred_element_type=F32)

    def scan(half):
        cr, ci = state_cols(half)
        lr = jnp.broadcast_to(lr_ref[0, :, cr], (nb, hw))
        li = jnp.broadcast_to(li_ref[0, :, cr], (nb, hw))
        xr, xi = st_ref[:, cr], st_ref[:, ci]
        for tt in range(rb // nb):
            rows = pl.ds(tt * nb, nb)
            xr, xi = (lr * xr - li * xi + buf_ref[rows, cr], lr * xi + li * xr + buf_ref[rows, ci])
            buf_ref[rows, cr] = xr
            buf_ref[rows, ci] = xi
        st_ref[:, cr] = xr
        st_ref[:, ci] = xi

    def c_proj(half):
        cr, ci = state_cols(half)
        srows = pl.ds(half * hw, hw)
        return (jnp.dot(buf_ref[:, cr].astype(BF16), cre_ref[0, srows, :], preferred_element_type=F32)
                - jnp.dot(buf_ref[:, ci].astype(BF16), cim_ref[0, srows, :],
                          preferred_element_type=F32))

    b_proj(0)
    y = None
    for half in range(nhalf):
        if half + 1 < nhalf:
            b_proj(half + 1)
        scan(half)
        part = c_proj(half)
        y = part if y is None else y + part

    y = y + dsk_ref[0] * u_ref[...]
    o_ref[...] = _gelu_tanh(y).astype(BF16)


def _ssm_core(u, bmat, lam_re, lam_im, cre, cim, d_skip, nb, rb=2048):
    m, dm = u.shape
    nchunk = dm // SSM_CHUNK
    s = SSM_CHUNK_STATES
    vmem = (2 * rb * SSM_CHUNK * 4 + 2 * SSM_CHUNK * 2 * s * 2 + 4 * s * SSM_CHUNK * 2
            + 2 * rb * SSM_CHUNK * 2 + rb * 2 * s * 4 + nb * 2 * s * 4
            + 3 * rb * SSM_SCAN_LANES * 4 + 4 * MIB)
    return pl.pallas_call(
        functools.partial(_ssm_core_kernel, nb=nb, rb=rb),
        grid=(nchunk, m // rb),
        in_specs=[pl.BlockSpec((rb, SSM_CHUNK), lambda c, i: (i, c)),
                  pl.BlockSpec((1, SSM_CHUNK, 2 * s), lambda c, i: (c, 0, 0)),
                  pl.BlockSpec((1, 1, s), lambda c, i: (c, 0, 0)),
                  pl.BlockSpec((1, 1, s), lambda c, i: (c, 0, 0)),
                  pl.BlockSpec((1, s, SSM_CHUNK), lambda c, i: (c, 0, 0)),
                  pl.BlockSpec((1, s, SSM_CHUNK), lambda c, i: (c, 0, 0)),
                  pl.BlockSpec((1, 1, SSM_CHUNK), lambda c, i: (c, 0, 0))],
        out_specs=pl.BlockSpec((rb, SSM_CHUNK), lambda c, i: (i, c)),
        out_shape=jax.ShapeDtypeStruct((m, dm), BF16),
        scratch_shapes=[pltpu.VMEM((nb, 2 * s), F32), pltpu.VMEM((rb, 2 * s), F32)],
        compiler_params=_params(("parallel", "arbitrary"), vmem),
        name="ssm_core",
    )(u, bmat, lam_re, lam_im, cre, cim, d_skip)


def _ssm_gate_kernel(g_ref, w_ref, o_ref):
    g = g_ref[...]
    z = jnp.dot(g, w_ref[...], preferred_element_type=F32)
    o_ref[...] = (g.astype(F32) * _sigmoid(z)).astype(BF16)


def _ssm_gate(g, w, bm=1024):
    m, dm = g.shape
    w, layer = w
    vmem = 4 * bm * dm * 2 + 2 * dm * dm * 2 + 4 * bm * dm * 4 + 4 * MIB
    return pl.pallas_call(
        _ssm_gate_kernel,
        grid=(m // bm,),
        in_specs=[pl.BlockSpec((bm, dm), lambda i: (i, 0)),
                  pl.BlockSpec((None, dm, dm), lambda i: (layer, 0, 0))],
        out_specs=pl.BlockSpec((bm, dm), lambda i: (i, 0)),
        out_shape=jax.ShapeDtypeStruct((m, dm), BF16),
        compiler_params=_params(("parallel",), vmem),
        name="ssm_glu",
    )(g, w)


def _block_diag_chunks(a):
    g, r, s = a.shape
    n = SSM_CHUNK_GROUPS
    eye = jnp.eye(n, dtype=a.dtype)
    out = a.reshape(g // n, n, r, 1, s) * eye.reshape(1, n, 1, n, 1)
    return out.reshape(g // n, n * r, n * s)


def _s5_mixer(x, g, sc, sh, gate, w_in, log_step, a_re, a_im, b_re, b_im, c_re, c_im, d_skip,
              w_gate, w_out):
    b, t, dm = x.shape
    ng = dm // SSM_GROUP
    nchunk = dm // SSM_CHUNK
    s = SSM_CHUNK_STATES
    lam_re, lam_im, bb_re, bb_im = _ssm_discretize(log_step, a_re, a_im, b_re, b_im)
    bmat = jnp.concatenate([_block_diag_chunks(bb_re.transpose(0, 2, 1)),
                            _block_diag_chunks(bb_im.transpose(0, 2, 1))], axis=-1).astype(BF16)
    cre = _block_diag_chunks(c_re.transpose(0, 2, 1)).astype(BF16)
    cim = _block_diag_chunks(c_im.transpose(0, 2, 1)).astype(BF16)
    lam_re = lam_re.reshape(nchunk, 1, s)
    lam_im = lam_im.reshape(nchunk, 1, s)
    u = _ssm_in(x, g, sc, sh, w_in)
    gq = _ssm_core(u, bmat, lam_re, lam_im, cre, cim, d_skip.reshape(nchunk, 1, SSM_CHUNK), nb=b)
    gq = _ssm_gate(gq, w_gate)
    return _mm_res(gq, w_out, x, gate, bm=32 * b, bk=dm, time_batch=True)


def kernel(x, c, ada_w, ada_b, norm1_g, norm2_g, attn_w_in, attn_w_out, ssm_w_in, ssm_log_step, ssm_a_re, ssm_a_im, ssm_b_re, ssm_b_im, ssm_c_re, ssm_c_im, ssm_d, ssm_w_gate, ssm_w_out, ffn_w_up, ffn_conv_w, ffn_conv_b, ffn_w_down, final_norm_g):
    b, t, dm = x.shape
    depth = ada_w.shape[0]
    hd = N_HEADS * HEAD_DIM
    slopes = 2.0 ** (-8.0 * jnp.arange(1, N_HEADS + 1, dtype=F32) / N_HEADS)

    attn_w_in, attn_w_out, ssm_w_in, ssm_w_gate, ssm_w_out, ffn_w_up, ffn_w_down = (
        w.astype(BF16) for w in (attn_w_in, attn_w_out, ssm_w_in, ssm_w_gate, ssm_w_out,
                                 ffn_w_up, ffn_w_down))

    mod = _ada(c, ada_w, ada_b).reshape(depth, b, 6, 1, dm)
    for i in range(depth):
        j = i // 2
        sh1, sc1, g1, sh2, sc2, g2 = (mod[i, :, q] for q in range(6))
        n1 = norm1_g[i].reshape(1, dm)
        if i % 2 == 0:
            groups = []
            for gi, (_, d) in enumerate(DILATED_GROUPS):
                if d == 1:
                    qkv, v_shared = _attn_proj(x, n1, sc1, sh1, (attn_w_in, j), gi, d)
                    groups.append(((qkv, 0), (qkv, N_HEADS), (qkv, 2 * N_HEADS)))
                else:
                    qk = _attn_proj(x, n1, sc1, sh1, (attn_w_in, j), gi, d)
                    groups.append(((qk, 0), (qk, N_HEADS), (v_shared[gi - 1], 0)))
            o = _attention(groups, slopes)
            x = _mm_res(o.reshape(b * t, hd), (attn_w_out, j), x, g1, bm=512, bk=hd)
        else:
            x = _s5_mixer(x, n1, sc1, sh1, g1, (ssm_w_in, j), ssm_log_step[j],
                          ssm_a_re[j], ssm_a_im[j], ssm_b_re[j], ssm_b_im[j], ssm_c_re[j],
                          ssm_c_im[j], ssm_d[j], (ssm_w_gate, j), (ssm_w_out, j))
        a = _ffn_up(x, norm2_g[i].reshape(1, dm), sc2, sh2, (ffn_w_up, i),
                    ffn_conv_w[i], ffn_conv_b[i].reshape(1, 2 * D_FF))
        x = _mm_res(a.reshape(b * t, D_FF), (ffn_w_down, i), x, g2, bm=512, bk=D_FF, bn=dm // 2)
    return _final_norm(x, final_norm_g.reshape(1, dm))
```

```python
import functools
import math

import jax
import jax.numpy as jnp
from jax import lax
from jax.experimental import pallas as pl
from jax.experimental.pallas import tpu as pltpu

F32 = jnp.float32
BF16 = jnp.bfloat16

D_MODEL = 2048
N_HEADS = 16
HEAD_DIM = 128
DILATED_GROUPS = ((128, 1), (512, 4), (2048, 16))
ATTN_BLOCK = 128
SSM_GROUP = 16
SSM_STATE = 64
D_FF = 5632
CONV_W = 3
EPS = 1e-6

V7X_LANES = 128
V7X_BF16_SUBLANES = 16
V7X_MXU_COLS = 256
SAFE_STRIDE = 4
V7X_SCOPED_VMEM_CAP_BYTES = 60000 * 1024
MIB = 1024 * 1024

SSM_CHUNK = 256
SSM_CHUNK_GROUPS = SSM_CHUNK // SSM_GROUP
SSM_CHUNK_STATES = SSM_CHUNK_GROUPS * SSM_STATE
SSM_SCAN_LANES = 512
LHS_K_CHUNKS = (512, 1536)
CONV_HALO = 16


def _params(semantics, vmem_bytes):
    return pltpu.CompilerParams(
        dimension_semantics=semantics,
        vmem_limit_bytes=int(min(vmem_bytes, V7X_SCOPED_VMEM_CAP_BYTES)))


def _sigmoid(x):
    return 1.0 / (1.0 + jnp.exp(-x))


def _norm_mod(x, g, sc, sh):
    ms = jnp.mean(x * x, axis=-1, keepdims=True)
    y = x * lax.rsqrt(ms + EPS)
    return (y * g) * (1.0 + sc) + sh


def _ada_kernel(c_ref, w_ref, b_ref, o_ref):
    c = c_ref[...]
    cs = (c * _sigmoid(c)).astype(BF16)
    o_ref[0] = jnp.dot(cs, w_ref[0].astype(BF16), preferred_element_type=F32) + b_ref[0]


def _ada(c, ada_w, ada_b, bn=1024):
    depth, d, n = ada_w.shape
    b = c.shape[0]
    vmem = 2 * (d * bn * 4) + d * bn * 2 + 4 * b * (d + bn) * 4 + 4 * MIB
    return pl.pallas_call(
        _ada_kernel,
        grid=(depth, n // bn),
        in_specs=[pl.BlockSpec((b, d), lambda l, j: (0, 0)),
                  pl.BlockSpec((1, d, bn), lambda l, j: (l, 0, j)),
                  pl.BlockSpec((1, 1, bn), lambda l, j: (l, 0, j))],
        out_specs=pl.BlockSpec((1, b, bn), lambda l, j: (l, 0, j)),
        out_shape=jax.ShapeDtypeStruct((depth, b, n), F32),
        compiler_params=_params(("parallel", "parallel"), vmem),
        name="ada_mod",
    )(c, ada_w, ada_b.reshape(depth, 1, n))


def _gather_classes(stage_ref, stage2_ref, chunk, nrows, dils, emit):
    quarter = nrows // SAFE_STRIDE
    for r0 in range(SAFE_STRIDE):
        rows = stage_ref[chunk, pl.ds(r0, quarter, stride=SAFE_STRIDE), :]
        if SAFE_STRIDE in dils:
            emit(SAFE_STRIDE, r0, rows)
        if any(d > SAFE_STRIDE for d in dils):
            stage2_ref[chunk, pl.ds(r0 * quarter, quarter), :] = rows
    for d in dils:
        if d > SAFE_STRIDE:
            step = d // SAFE_STRIDE
            assert step <= SAFE_STRIDE
            for r0 in range(SAFE_STRIDE):
                for r1 in range(step):
                    emit(d, SAFE_STRIDE * r1 + r0,
                         stage2_ref[chunk, pl.ds(r0 * quarter + r1, nrows // d, stride=step), :])


def _scatter_class(stage2_ref, chunk, nrows, d, r, rows):
    step = d // SAFE_STRIDE
    assert SAFE_STRIDE < d and step <= SAFE_STRIDE
    r0, r1 = r % SAFE_STRIDE, r // SAFE_STRIDE
    stage2_ref[chunk, pl.ds(r0 * (nrows // SAFE_STRIDE) + r1, nrows // d, stride=step), :] = rows


def _scatter_finish(stage_ref, stage2_ref, chunk, nrows):
    quarter = nrows // SAFE_STRIDE
    for r0 in range(SAFE_STRIDE):
        stage_ref[chunk, pl.ds(r0, quarter, stride=SAFE_STRIDE), :] = (
            stage2_ref[chunk, pl.ds(r0 * quarter, quarter), :])


def _attn_proj_kernel(x_ref, g_ref, sc_ref, sh_ref, w_ref, o_ref, *rest, d, bm, bn, rc, kc, v_dils,
                      v_first_block):
    nv = len(v_dils)
    v_refs, lhs_ref, inv_ref, stage_ref = rest[:nv], rest[nv], rest[nv + 1], rest[nv + 2]
    stage2_ref = rest[nv + 3] if len(rest) > nv + 3 else None
    j = pl.program_id(2)
    rpc = bm // d
    dm = x_ref.shape[-1]
    heads = bn // HEAD_DIM

    def write_out(acc):
        for hh in range(heads):
            for r in range(d):
                o_ref[0, hh, r] = acc[r * rpc:(r + 1) * rpc,
                                      hh * HEAD_DIM:(hh + 1) * HEAD_DIM].astype(BF16)

    @pl.when(j == 0)
    def _():
        g, sc, sh = g_ref[...], sc_ref[0], sh_ref[0]
        for c in range(bm // rc):
            rows = pl.ds(c * rc, rc)
            x = x_ref[0, rows, :]
            inv = lax.rsqrt(jnp.mean(x * x, axis=-1, keepdims=True) + EPS)
            inv_ref[rows, :] = jnp.broadcast_to(inv, (rc, V7X_LANES))
        acc = None
        assert sum(kc) == dm
        for k0, kw in zip((sum(kc[:n]) for n in range(len(kc))), kc):
            for lc in range(k0 // V7X_LANES, (k0 + kw) // V7X_LANES):
                lanes = pl.ds(lc * V7X_LANES, V7X_LANES)
                lo, hi = lc * V7X_LANES, (lc + 1) * V7X_LANES
                for c in range(bm // rc):
                    rows = pl.ds(c * rc, rc)
                    h = (((x_ref[0, rows, lanes] * inv_ref[rows, :]) * g[:, lo:hi])
                         * (1.0 + sc[:, lo:hi]) + sh[:, lo:hi])
                    if d == 1:
                        lhs_ref[rows, lanes] = h.astype(BF16)
                    else:
                        stage_ref[lc, rows, :] = h
                if d > 1:
                    def to_lhs(_, r, rows, lanes=lanes):
                        lhs_ref[pl.ds(r * rpc, rpc), lanes] = rows.astype(BF16)
                    _gather_classes(stage_ref, stage2_ref, lc, bm, (d,), to_lhs)
            klanes = pl.ds(k0, kw)
            part = jnp.dot(lhs_ref[:, klanes], w_ref[klanes, :], preferred_element_type=F32)
            acc = part if acc is None else acc + part
        write_out(acc)

    @pl.when(j > 0)
    def _():
        acc = jnp.dot(lhs_ref[...], w_ref[...], preferred_element_type=F32)
        write_out(acc)
        if v_refs:
            @pl.when(j >= v_first_block)
            def _():
                for hh in range(heads):
                    stage_ref[hh] = acc[:, hh * HEAD_DIM:(hh + 1) * HEAD_DIM]
                for hh in range(heads):
                    def to_v(dd, r, rows, hh=hh):
                        v_refs[v_dils.index(dd)][0, hh, r] = rows.astype(BF16)
                    _gather_classes(stage_ref, stage2_ref, hh, bm, v_dils, to_v)


def _attn_proj(x, g, sc, sh, w, gi, d, bm=1024, bn=1024):
    b, t, dm = x.shape
    w, layer = w
    hd = N_HEADS * HEAD_DIM
    parts = 3 if d == 1 else 2
    n = parts * hd
    per_part = hd // bn
    n_groups = len(DILATED_GROUPS)
    part_start = (gi * per_part, (n_groups + gi) * per_part, 2 * n_groups * per_part)
    v_dils = tuple(dd for _, dd in DILATED_GROUPS if dd > 1) if d == 1 else ()
    v_first_block = 2 * per_part

    def w_block(bi, i, j):
        part = j // per_part
        start = jnp.where(part == 0, part_start[0],
                          jnp.where(part == 1, part_start[1], part_start[2]))
        return (layer, 0, start + j % per_part)

    rpc = bm // d
    rc = 256
    heads = bn // HEAD_DIM
    assert rpc % V7X_BF16_SUBLANES == 0 and bm % rc == 0 and t % bm == 0 and hd % bn == 0
    assert all((bm // dd) % V7X_BF16_SUBLANES == 0 for dd in v_dils)
    stage_chunks = heads if d == 1 else dm // V7X_LANES
    n_stages = 2 if max(v_dils + (d,)) > SAFE_STRIDE else 1
    scratch = ([pltpu.VMEM((bm, dm), BF16), pltpu.VMEM((bm, V7X_LANES), F32)]
               + [pltpu.VMEM((stage_chunks, bm, V7X_LANES), F32)] * n_stages)
    out_specs = [pl.BlockSpec((1, heads, d, rpc, HEAD_DIM), lambda bi, i, j: (bi, j, 0, i, 0))]
    out_shape = [jax.ShapeDtypeStruct((b, n // HEAD_DIM, d, t // d, HEAD_DIM), BF16)]
    for dd in v_dils:
        out_specs.append(pl.BlockSpec(
            (1, heads, dd, bm // dd, HEAD_DIM),
            lambda bi, i, j: (bi, jnp.maximum(j - v_first_block, 0), 0, i, 0)))
        out_shape.append(jax.ShapeDtypeStruct((b, N_HEADS, dd, t // dd, HEAD_DIM), BF16))
    vmem = (2 * bm * dm * 4 + bm * dm * 2 + n_stages * stage_chunks * bm * V7X_LANES * 4
            + 2 * dm * bn * 2
            + 2 * (1 + len(v_dils)) * bm * bn * 2 + bm * bn * 4 + 4 * rc * dm * 4 + 4 * MIB)
    outs = pl.pallas_call(
        functools.partial(_attn_proj_kernel, d=d, bm=bm, bn=bn, rc=rc, kc=LHS_K_CHUNKS, v_dils=v_dils,
                          v_first_block=v_first_block),
        grid=(b, t // bm, n // bn),
        in_specs=[pl.BlockSpec((1, bm, dm), lambda bi, i, j: (bi, i, 0)),
                  pl.BlockSpec((1, dm), lambda bi, i, j: (0, 0)),
                  pl.BlockSpec((1, 1, dm), lambda bi, i, j: (bi, 0, 0)),
                  pl.BlockSpec((1, 1, dm), lambda bi, i, j: (bi, 0, 0)),
                  pl.BlockSpec((None, dm, bn), w_block)],
        out_specs=out_specs,
        out_shape=out_shape,
        scratch_shapes=scratch,
        compiler_params=_params(("parallel", "parallel", "arbitrary"), vmem),
        name=f"attn_proj_d{d}",
    )(x, g, sc, sh, w)
    outs = [o.reshape(b, o.shape[1], t, HEAD_DIM) for o in outs]
    return (outs[0], outs[1:]) if d == 1 else outs[0]


def _attn_kernel(slopes_ref, q1, k1, v1, q4, k4, v4, q16, k16, v16, o_ref, o_scr, l_scr, *, t,
                 stage_batch):
    blk = ATTN_BLOCK
    slope = slopes_ref[pl.program_id(1)]
    scale = HEAD_DIM ** -0.5
    qi = lax.broadcasted_iota(jnp.int32, (blk, 2 * blk), 0)
    kj = lax.broadcasted_iota(jnp.int32, (blk, 2 * blk), 1)
    dist = blk + qi - kj
    in_band = (dist >= 0) & (dist <= blk)

    refs = ((q1, k1, v1), (q4, k4, v4), (q16, k16, v16))
    stage_slot = len(DILATED_GROUPS)
    quarter = t // SAFE_STRIDE
    assert sum(d > SAFE_STRIDE for _, d in DILATED_GROUPS) <= 1
    work = []
    for gi, ((_, d), (q_ref, k_ref, v_ref)) in enumerate(zip(DILATED_GROUPS, refs)):
        cls_len = t // d
        nb = cls_len // blk
        bias_full = jnp.where(in_band, -slope * (dist * d).astype(F32), -jnp.inf)
        bias_first = bias_full[:, blk:]

        for r in range(d):
            base = r * cls_len
            for n in range(nb):
                if d == 1:
                    dst = (gi, pl.ds(n * blk, blk))
                elif d <= SAFE_STRIDE:
                    dst = (gi, pl.ds(n * blk * d + r, blk, stride=d))
                else:
                    step = d // SAFE_STRIDE
                    r0, r1 = r % SAFE_STRIDE, r // SAFE_STRIDE
                    dst = (stage_slot, pl.ds(r0 * quarter + step * n * blk + r1, blk, stride=step))
                q_rows = pl.ds(base + n * blk, blk)
                if n == 0:
                    work.append((q_ref, k_ref, v_ref, q_rows, pl.ds(base, blk), bias_first, dst))
                else:
                    work.append((q_ref, k_ref, v_ref, q_rows,
                                 pl.ds(base + (n - 1) * blk, 2 * blk), bias_full, dst))

    for w0 in range(0, len(work), stage_batch):
        batch = work[w0:w0 + stage_batch]
        ss = [lax.dot_general(q_ref[0, 0, qr, :], k_ref[0, 0, kr, :], (((1,), (1,)), ((), ())),
                              preferred_element_type=F32) * scale + bias
              for (q_ref, k_ref, _, qr, kr, bias, _) in batch]
        ms = [jnp.max(s, axis=-1, keepdims=True) for s in ss]
        es = [jnp.exp(s - m).astype(BF16) for s, m in zip(ss, ms)]
        outs = []
        for e, (_, _, v_ref, _, kr, _, _) in zip(es, batch):
            vv = v_ref[0, 0, kr, :]
            outs.append(jnp.dot(e, jnp.concatenate([vv, jnp.ones_like(vv)], axis=1),
                                preferred_element_type=F32))
        for o, m, (_, _, _, _, _, _, (slot, rows)) in zip(outs, ms, batch):
            den = o[:, HEAD_DIM:]
            o_scr[slot, rows, :] = o[:, :HEAD_DIM] / den
            l_scr[slot, rows, :] = m + jnp.log(den)

    for gi, (_, d) in enumerate(DILATED_GROUPS):
        if d > SAFE_STRIDE:
            for scr in (o_scr, l_scr):
                for r0 in range(SAFE_STRIDE):
                    for c in range(quarter // blk):
                        scr[gi, pl.ds(r0 + c * blk * SAFE_STRIDE, blk, stride=SAFE_STRIDE), :] = (
                            scr[stage_slot, pl.ds(r0 * quarter + c * blk, blk), :])

    rc = 256

    def combine(c, carry):
        rows = pl.ds(pl.multiple_of(c * rc, rc), rc)
        l0, l1, l2 = l_scr[0, rows, :], l_scr[1, rows, :], l_scr[2, rows, :]
        mx = jnp.maximum(jnp.maximum(l0, l1), l2)
        w0, w1, w2 = jnp.exp(l0 - mx), jnp.exp(l1 - mx), jnp.exp(l2 - mx)
        o = (w0 * o_scr[0, rows, :] + w1 * o_scr[1, rows, :] + w2 * o_scr[2, rows, :]) \
            / (w0 + w1 + w2)
        o_ref[0, rows, :] = o.astype(BF16)
        return carry

    lax.fori_loop(0, t // rc, combine, 0, unroll=2)


def _attention(groups, slopes, stage_batch=16):
    b, _, t, dh = groups[0][0][0].shape
    h = N_HEADS
    in_specs = [pl.BlockSpec(memory_space=pltpu.SMEM)]
    args = [slopes]
    for group in groups:
        for arr, head0 in group:
            in_specs.append(pl.BlockSpec(
                (1, 1, t, dh), functools.partial(lambda bi, hi, head0: (bi, head0 + hi, 0, 0),
                                                 head0=head0)))
            args.append(arr)
    vmem = 2 * 9 * t * dh * 2 + 2 * t * dh * 2 + 8 * t * dh * 4 + 8 * MIB
    return pl.pallas_call(
        functools.partial(_attn_kernel, t=t, stage_batch=stage_batch),
        grid=(b, h),
        in_specs=in_specs,
        out_specs=pl.BlockSpec((1, t, dh), lambda bi, hi: (bi, 0, hi)),
        out_shape=jax.ShapeDtypeStruct((b, t, h * dh), BF16),
        scratch_shapes=[pltpu.VMEM((len(groups) + 1, t, dh), F32)] * 2,
        compiler_params=_params(("parallel", "parallel"), vmem),
        name="dilated_attn",
    )(*args)


def _mm_res_kernel(lhs_ref, w_ref, x_ref, gate_ref, o_ref, *scratch, nk, tb_rows, rc):
    acc_ref = accs_ref = None
    if tb_rows is not None:
        acc_ref, accs_ref, accs2_ref = scratch
    elif nk > 1:
        acc_ref, = scratch
    k = pl.program_id(2)
    part = jnp.dot(lhs_ref[...], w_ref[...], preferred_element_type=F32)

    def finish(xs, gate, res):
        return xs + gate * res

    if acc_ref is None:
        o_ref[...] = finish(x_ref[...], gate_ref[0], part)
        return

    if nk == 1:
        acc_ref[...] = part
    else:
        @pl.when(k == 0)
        def _():
            acc_ref[...] = part

        @pl.when(k > 0)
        def _():
            acc_ref[...] += part

    @pl.when(k == nk - 1)
    def _():
        if tb_rows is None:
            gate = gate_ref[0]
            bm = acc_ref.shape[0]
            for c in range(bm // rc):
                rows = pl.ds(c * rc, rc)
                o_ref[rows, :] = finish(x_ref[rows, :], gate, acc_ref[rows, :])
        else:
            nb, bt = tb_rows
            nlc = acc_ref.shape[-1] // V7X_LANES
            for lc in range(nlc):
                lanes = pl.ds(lc * V7X_LANES, V7X_LANES)
                accs_ref[lc] = acc_ref[:, lanes]

                def put(_, bi, res, lanes=lanes):
                    o_ref[bi, :, lanes] = finish(x_ref[bi, :, lanes], gate_ref[bi, :, lanes], res)

                _gather_classes(accs_ref, accs2_ref, lc, nb * bt, (nb,), put)


def _mm_res(lhs, w, x, gate, *, bm, bk, bn=None, time_batch=False):
    b, t, dm = x.shape
    w, layer = w
    m, kdim = lhs.shape
    bn = dm if bn is None else bn
    assert m == b * t and kdim % bk == 0 and m % bm == 0 and dm % bn == 0
    nk = kdim // bk
    if time_batch:
        assert bn == dm and b > SAFE_STRIDE
        bt = bm // b
        x_in, x_spec = x, pl.BlockSpec((b, bt, dm), lambda j, i, k: (0, i, 0))
        gate_spec = pl.BlockSpec((b, 1, dm), lambda j, i, k: (0, 0, 0))
        out_shape = jax.ShapeDtypeStruct((b, t, dm), F32)
        tb_rows = (b, bt)
        scratch = ([pltpu.VMEM((bm, dm), F32)]
                   + [pltpu.VMEM((dm // V7X_LANES, bm, V7X_LANES), F32)] * 2)
    else:
        assert t % bm == 0
        x_in, x_spec = x.reshape(m, dm), pl.BlockSpec((bm, bn), lambda j, i, k: (i, j))
        gate_spec = pl.BlockSpec((1, 1, bn), lambda j, i, k: (i * bm // t, 0, j))
        out_shape = jax.ShapeDtypeStruct((m, dm), F32)
        tb_rows = None
        scratch = [pltpu.VMEM((bm, bn), F32)] if nk > 1 else []
    in_specs = [pl.BlockSpec((bm, bk), lambda j, i, k: (i, k)),
                pl.BlockSpec((None, bk, bn), lambda j, i, k: (layer, k, j)),
                x_spec, gate_spec]
    args = [lhs, w, x_in, gate]
    rc = min(bm, 256)
    vmem = (2 * bm * bk * 2 + 2 * bk * bn * 2 + 4 * bm * bn * 4 + (2 + 2 * time_batch) * bm * bn * 4
            + 6 * rc * bn * 4 + 4 * MIB)
    out = pl.pallas_call(
        functools.partial(_mm_res_kernel, nk=nk, tb_rows=tb_rows, rc=rc),
        grid=(dm // bn, m // bm, nk),
        in_specs=in_specs,
        out_specs=x_spec,
        out_shape=out_shape,
        scratch_shapes=scratch,
        compiler_params=_params(("parallel", "parallel", "arbitrary"), vmem),
        name="mm_res_tb" if time_batch else "mm_res",
    )(*args)
    return out.reshape(b, t, dm)


def _final_norm_kernel(x_ref, g_ref, o_ref):
    x = x_ref[...]
    ms = jnp.mean(x * x, axis=-1, keepdims=True)
    o_ref[...] = (x * lax.rsqrt(ms + EPS)) * g_ref[...]


def _final_norm(x, g, bm=1024):
    b, t, dm = x.shape
    m = b * t
    out = pl.pallas_call(
        _final_norm_kernel,
        grid=(m // bm,),
        in_specs=[pl.BlockSpec((bm, dm), lambda i: (i, 0)),
                  pl.BlockSpec((1, dm), lambda i: (0, 0))],
        out_specs=pl.BlockSpec((bm, dm), lambda i: (i, 0)),
        out_shape=jax.ShapeDtypeStruct((m, dm), F32),
        compiler_params=_params(("parallel",), 8 * bm * dm * 4 + 4 * MIB),
        name="final_norm",
    )(x.reshape(m, dm), g)
    return out.reshape(b, t, dm)


def _ffn_up_kernel(x_ref, xh_ref, g_ref, sc_ref, sh_ref, wg_ref, wu_ref, cwg_ref, cwu_ref,
                   cbg_ref, cbu_ref, o_ref, lhs_ref, inv_ref, *, bm, rc, kc):
    i, j = pl.program_id(1), pl.program_id(2)
    dm = x_ref.shape[-1]
    bn = o_ref.shape[-1]
    up_cols = [pl.ds(s * V7X_MXU_COLS, V7X_MXU_COLS) for s in range(bn // V7X_MXU_COLS)]

    def conv(a, cw_ref, cb_ref, cols):
        cw = cw_ref[:, cols]
        taps = pltpu.roll(cw[0:1] * a, 1, axis=0) + cw[1:2] * a
        taps = pltpu.roll(taps, 1, axis=0) + cw[2:3] * a
        return cb_ref[:, cols] + taps[CONV_HALO:]

    def glu(act, a_up, s):
        up = conv(a_up, cwu_ref, cbu_ref, up_cols[s])
        o_ref[0, :, up_cols[s]] = (act[:, s * V7X_MXU_COLS:(s + 1) * V7X_MXU_COLS] * up).astype(BF16)

    @pl.when(j == 0)
    def _():
        g, sc, sh = g_ref[...], sc_ref[0], sh_ref[0]

        def inv_rms(x):
            return jnp.broadcast_to(lax.rsqrt(jnp.mean(x * x, axis=-1, keepdims=True) + EPS),
                                    (x.shape[0], V7X_LANES))

        inv_ref[pl.ds(0, CONV_HALO), :] = inv_rms(xh_ref[0])
        for c in range(bm // rc):
            inv_ref[pl.ds(CONV_HALO + c * rc, rc), :] = inv_rms(x_ref[0, pl.ds(c * rc, rc), :])

        a_g, a_u = None, [None] * len(up_cols)
        assert sum(kc) == dm
        for k0, kw in zip((sum(kc[:n]) for n in range(len(kc))), kc):
            for lc in range(k0 // V7X_LANES, (k0 + kw) // V7X_LANES):
                lanes = pl.ds(lc * V7X_LANES, V7X_LANES)
                lo, hi = lc * V7X_LANES, (lc + 1) * V7X_LANES

                def norm_mod(x, inv):
                    return ((x * inv) * g[:, lo:hi]) * (1.0 + sc[:, lo:hi]) + sh[:, lo:hi]

                halo = norm_mod(xh_ref[0, :, lanes], inv_ref[pl.ds(0, CONV_HALO), :])
                lhs_ref[pl.ds(0, CONV_HALO), lanes] = jnp.where(i == 0, 0.0, halo).astype(BF16)
                for c in range(bm // rc):
                    rows = pl.ds(CONV_HALO + c * rc, rc)
                    lhs_ref[rows, lanes] = norm_mod(x_ref[0, pl.ds(c * rc, rc), lanes],
                                                    inv_ref[rows, :]).astype(BF16)
            klanes = pl.ds(k0, kw)
            lk = lhs_ref[:, klanes]
            part = jnp.dot(lk, wg_ref[klanes, :], preferred_element_type=F32)
            a_g = part if a_g is None else a_g + part
            for s, cols in enumerate(up_cols):
                part = jnp.dot(lk, wu_ref[klanes, cols], preferred_element_type=F32)
                a_u[s] = part if a_u[s] is None else a_u[s] + part
        gate = conv(a_g, cwg_ref, cbg_ref, pl.ds(0, bn))
        act = gate * _sigmoid(gate)
        for s in range(len(up_cols)):
            glu(act, a_u[s], s)

    @pl.when(j > 0)
    def _():
        lhs = lhs_ref[...]
        gate = conv(jnp.dot(lhs, wg_ref[...], preferred_element_type=F32), cwg_ref, cbg_ref,
                    pl.ds(0, bn))
        act = gate * _sigmoid(gate)
        for s, cols in enumerate(up_cols):
            glu(act, jnp.dot(lhs, wu_ref[:, cols], preferred_element_type=F32), s)


def _ffn_up(x, g, sc, sh, w_up, conv_w, conv_b, bm=1024, bn=512):
    b, t, dm = x.shape
    w_up, layer = w_up
    f = w_up.shape[-1] // 2
    assert t % bm == 0 and f % bn == 0 and bm % CONV_HALO == 0
    nj = f // bn
    rc = 256
    vmem = (2 * bm * dm * 4 + (bm + CONV_HALO) * dm * 2 + 4 * dm * bn * 2 + 2 * bm * bn * 2
            + 8 * (bm + CONV_HALO) * bn * 4 + 4 * rc * dm * 4 + 4 * MIB)
    halo_blocks = bm // CONV_HALO
    return pl.pallas_call(
        functools.partial(_ffn_up_kernel, bm=bm, rc=rc, kc=LHS_K_CHUNKS),
        grid=(b, t // bm, nj),
        in_specs=[pl.BlockSpec((1, bm, dm), lambda bi, i, j: (bi, i, 0)),
                  pl.BlockSpec((1, CONV_HALO, dm),
                               lambda bi, i, j: (bi, jnp.maximum(i * halo_blocks - 1, 0), 0)),
                  pl.BlockSpec((1, dm), lambda bi, i, j: (0, 0)),
                  pl.BlockSpec((1, 1, dm), lambda bi, i, j: (bi, 0, 0)),
                  pl.BlockSpec((1, 1, dm), lambda bi, i, j: (bi, 0, 0)),
                  pl.BlockSpec((None, dm, bn), lambda bi, i, j: (layer, 0, j)),
                  pl.BlockSpec((None, dm, bn), lambda bi, i, j: (layer, 0, nj + j)),
                  pl.BlockSpec((CONV_W, bn), lambda bi, i, j: (0, j)),
                  pl.BlockSpec((CONV_W, bn), lambda bi, i, j: (0, nj + j)),
                  pl.BlockSpec((1, bn), lambda bi, i, j: (0, j)),
                  pl.BlockSpec((1, bn), lambda bi, i, j: (0, nj + j))],
        out_specs=pl.BlockSpec((1, bm, bn), lambda bi, i, j: (bi, i, j)),
        out_shape=jax.ShapeDtypeStruct((b, t, f), BF16),
        scratch_shapes=[pltpu.VMEM((bm + CONV_HALO, dm), BF16),
                        pltpu.VMEM((bm + CONV_HALO, V7X_LANES), F32)],
        compiler_params=_params(("parallel", "parallel", "arbitrary"), vmem),
        name="ffn_up",
    )(x, x, g, sc, sh, w_up, w_up, conv_w, conv_w, conv_b, conv_b)


def _ssm_disc_kernel(ls_ref, ar_ref, ai_ref, br_ref, bi_ref, lr_ref, li_ref, bbr_ref, bbi_ref):
    step = jnp.exp(ls_ref[...])
    lr, li = ar_ref[...], ai_ref[...]
    mag = jnp.exp(lr * step)
    ang = li * step
    lb_re = mag * jnp.cos(ang)
    lb_im = mag * jnp.sin(ang)
    nr, ni = lb_re - 1.0, lb_im
    den = lr * lr + li * li
    coef_re = (nr * lr + ni * li) / den
    coef_im = (ni * lr - nr * li) / den
    br, bi = br_ref[...], bi_ref[...]
    lr_ref[...] = lb_re
    li_ref[...] = lb_im
    bbr_ref[...] = coef_re * br - coef_im * bi
    bbi_ref[...] = coef_re * bi + coef_im * br


def _ssm_discretize(log_step, a_re, a_im, b_re, b_im, bg=SSM_CHUNK_GROUPS):
    g, p, c = b_re.shape
    col = jax.ShapeDtypeStruct((g, p, 1), F32)
    full = jax.ShapeDtypeStruct((g, p, c), F32)
    col_spec = pl.BlockSpec((bg, p, 1), lambda i: (i, 0, 0))
    full_spec = pl.BlockSpec((bg, p, c), lambda i: (i, 0, 0))
    return pl.pallas_call(
        _ssm_disc_kernel,
        grid=(g // bg,),
        in_specs=[pl.BlockSpec((bg, 1, 1), lambda i: (i, 0, 0)), col_spec, col_spec,
                  full_spec, full_spec],
        out_specs=(col_spec, col_spec, full_spec, full_spec),
        out_shape=(col, col, full, full),
        compiler_params=_params(("parallel",), 32 * MIB),
        name="ssm_discretize",
    )(log_step.reshape(g, 1, 1), a_re.reshape(g, p, 1), a_im.reshape(g, p, 1), b_re, b_im)


def _ssm_in_kernel(x_ref, g_ref, sc_ref, sh_ref, w_ref, o_ref, hs_ref, hs2_ref, lhs_ref, *, nb, bt):
    g = g_ref[...]
    nlc = x_ref.shape[-1] // V7X_LANES
    for bi in range(nb):
        h = _norm_mod(x_ref[bi], g, sc_ref[bi], sh_ref[bi])
        for lc in range(nlc):
            _scatter_class(hs2_ref, lc, nb * bt, nb, bi, h[:, lc * V7X_LANES:(lc + 1) * V7X_LANES])
    for lc in range(nlc):
        _scatter_finish(hs_ref, hs2_ref, lc, nb * bt)
        lhs_ref[:, pl.ds(lc * V7X_LANES, V7X_LANES)] = hs_ref[lc].astype(BF16)
    o_ref[...] = jnp.dot(lhs_ref[...], w_ref[...], preferred_element_type=F32)


def _ssm_in(x, g, sc, sh, w, bt=32):
    b, t, dm = x.shape
    w, layer = w
    n = w.shape[-1]
    bm = bt * b
    vmem = (2 * bm * dm * 4 + 2 * bm * dm * 4 + bm * dm * 2 + 2 * dm * n * 2 + 2 * bm * n * 4
            + bm * n * 4 + 4 * MIB)
    return pl.pallas_call(
        functools.partial(_ssm_in_kernel, nb=b, bt=bt),
        grid=(t // bt,),
        in_specs=[pl.BlockSpec((b, bt, dm), lambda i: (0, i, 0)),
                  pl.BlockSpec((1, dm), lambda i: (0, 0)),
                  pl.BlockSpec((b, 1, dm), lambda i: (0, 0, 0)),
                  pl.BlockSpec((b, 1, dm), lambda i: (0, 0, 0)),
                  pl.BlockSpec((None, dm, n), lambda i: (layer, 0, 0))],
        out_specs=pl.BlockSpec((bm, n), lambda i: (i, 0)),
        out_shape=jax.ShapeDtypeStruct((t * b, n), F32),
        scratch_shapes=[pltpu.VMEM((dm // V7X_LANES, bm, V7X_LANES), F32),
                        pltpu.VMEM((dm // V7X_LANES, bm, V7X_LANES), F32),
                        pltpu.VMEM((bm, dm), BF16)],
        compiler_params=_params(("parallel",), vmem),
        name="ssm_in_proj",
    )(x, g, sc, sh, w)


def _gelu_tanh(y):
    return 0.5 * y * (1.0 + jnp.tanh(math.sqrt(2.0 / math.pi) * (y + 0.044715 * (y * y * y))))


def _ssm_core_kernel(u_ref, bmat_ref, lr_ref, li_ref, cre_ref, cim_ref, dsk_ref, o_ref,
                     st_ref, buf_ref, *, nb, rb):
    s = SSM_CHUNK_STATES
    hw = SSM_SCAN_LANES
    nhalf = s // hw

    @pl.when(pl.program_id(1) == 0)
    def _():
        st_ref[...] = jnp.zeros_like(st_ref)

    ub = u_ref[...].astype(BF16)

    def state_cols(half):
        return pl.ds(half * hw, hw), pl.ds(s + half * hw, hw)

    def b_proj(half):
        for cols in state_cols(half):
            buf_ref[:, cols] = jnp.dot(ub, bmat_ref[0, :, cols], preferred_element_type=F32)

    def scan(half):
        cr, ci = state_cols(half)
        lr = jnp.broadcast_to(lr_ref[0, :, cr], (nb, hw))
        li = jnp.broadcast_to(li_ref[0, :, cr], (nb, hw))
        xr, xi = st_ref[:, cr], st_ref[:, ci]
        for tt in range(rb // nb):
            rows = pl.ds(tt * nb, nb)
            xr, xi = (lr * xr - li * xi + buf_ref[rows, cr], lr * xi + li * xr + buf_ref[rows, ci])
            buf_ref[rows, cr] = xr
            buf_ref[rows, ci] = xi
        st_ref[:, cr] = xr
        st_ref[:, ci] = xi

    def c_proj(half):
        cr, ci = state_cols(half)
        srows = pl.ds(half * hw, hw)
        return (jnp.dot(buf_ref[:, cr].astype(BF16), cre_ref[0, srows, :], preferred_element_type=F32)
                - jnp.dot(buf_ref[:, ci].astype(BF16), cim_ref[0, srows, :],
                          preferred_element_type=F32))

    b_proj(0)
    y = None
    for half in range(nhalf):
        if half + 1 < nhalf:
            b_proj(half + 1)
        scan(half)
        part = c_proj(half)
        y = part if y is None else y + part

    y = y + dsk_ref[0] * u_ref[...]
    o_ref[...] = _gelu_tanh(y).astype(BF16)


def _ssm_core(u, bmat, lam_re, lam_im, cre, cim, d_skip, nb, rb=2048):
    m, dm = u.shape
    nchunk = dm // SSM_CHUNK
    s = SSM_CHUNK_STATES
    vmem = (2 * rb * SSM_CHUNK * 4 + 2 * SSM_CHUNK * 2 * s * 2 + 4 * s * SSM_CHUNK * 2
            + 2 * rb * SSM_CHUNK * 2 + rb * 2 * s * 4 + nb * 2 * s * 4
            + 3 * rb * SSM_SCAN_LANES * 4 + 4 * MIB)
    return pl.pallas_call(
        functools.partial(_ssm_core_kernel, nb=nb, rb=rb),
        grid=(nchunk, m // rb),
        in_specs=[pl.BlockSpec((rb, SSM_CHUNK), lambda c, i: (i, c)),
                  pl.BlockSpec((1, SSM_CHUNK, 2 * s), lambda c, i: (c, 0, 0)),
                  pl.BlockSpec((1, 1, s), lambda c, i: (c, 0, 0)),
                  pl.BlockSpec((1, 1, s), lambda c, i: (c, 0, 0)),
                  pl.BlockSpec((1, s, SSM_CHUNK), lambda c, i: (c, 0, 0)),
                  pl.BlockSpec((1, s, SSM_CHUNK), lambda c, i: (c, 0, 0)),
                  pl.BlockSpec((1, 1, SSM_CHUNK), lambda c, i: (c, 0, 0))],
        out_specs=pl.BlockSpec((rb, SSM_CHUNK), lambda c, i: (i, c)),
        out_shape=jax.ShapeDtypeStruct((m, dm), BF16),
        scratch_shapes=[pltpu.VMEM((nb, 2 * s), F32), pltpu.VMEM((rb, 2 * s), F32)],
        compiler_params=_params(("parallel", "arbitrary"), vmem),
        name="ssm_core",
    )(u, bmat, lam_re, lam_im, cre, cim, d_skip)


def _ssm_gate_kernel(g_ref, w_ref, o_ref):
    g = g_ref[...]
    z = jnp.dot(g, w_ref[...], preferred_element_type=F32)
    o_ref[...] = (g.astype(F32) * _sigmoid(z)).astype(BF16)


def _ssm_gate(g, w, bm=1024):
    m, dm = g.shape
    w, layer = w
    vmem = 4 * bm * dm * 2 + 2 * dm * dm * 2 + 4 * bm * dm * 4 + 4 * MIB
    return pl.pallas_call(
        _ssm_gate_kernel,
        grid=(m // bm,),
        in_specs=[pl.BlockSpec((bm, dm), lambda i: (i, 0)),
                  pl.BlockSpec((None, dm, dm), lambda i: (layer, 0, 0))],
        out_specs=pl.BlockSpec((bm, dm), lambda i: (i, 0)),
        out_shape=jax.ShapeDtypeStruct((m, dm), BF16),
        compiler_params=_params(("parallel",), vmem),
        name="ssm_glu",
    )(g, w)


def _block_diag_chunks(a):
    g, r, s = a.shape
    n = SSM_CHUNK_GROUPS
    eye = jnp.eye(n, dtype=a.dtype)
    out = a.reshape(g // n, n, r, 1, s) * eye.reshape(1, n, 1, n, 1)
    return out.reshape(g // n, n * r, n * s)


def _s5_mixer(x, g, sc, sh, gate, w_in, log_step, a_re, a_im, b_re, b_im, c_re, c_im, d_skip,
              w_gate, w_out):
    b, t, dm = x.shape
    ng = dm // SSM_GROUP
    nchunk = dm // SSM_CHUNK
    s = SSM_CHUNK_STATES
    lam_re, lam_im, bb_re, bb_im = _ssm_discretize(log_step, a_re, a_im, b_re, b_im)
    bmat = jnp.concatenate([_block_diag_chunks(bb_re.transpose(0, 2, 1)),
                            _block_diag_chunks(bb_im.transpose(0, 2, 1))], axis=-1).astype(BF16)
    cre = _block_diag_chunks(c_re.transpose(0, 2, 1)).astype(BF16)
    cim = _block_diag_chunks(c_im.transpose(0, 2, 1)).astype(BF16)
    lam_re = lam_re.reshape(nchunk, 1, s)
    lam_im = lam_im.reshape(nchunk, 1, s)
    u = _ssm_in(x, g, sc, sh, w_in)
    gq = _ssm_core(u, bmat, lam_re, lam_im, cre, cim, d_skip.reshape(nchunk, 1, SSM_CHUNK), nb=b)
    gq = _ssm_gate(gq, w_gate)
    return _mm_res(gq, w_out, x, gate, bm=32 * b, bk=dm, time_batch=True)


def kernel(x, c, ada_w, ada_b, norm1_g, norm2_g, attn_w_in, attn_w_out, ssm_w_in, ssm_log_step, ssm_a_re, ssm_a_im, ssm_b_re, ssm_b_im, ssm_c_re, ssm_c_im, ssm_d, ssm_w_gate, ssm_w_out, ffn_w_up, ffn_conv_w, ffn_conv_b, ffn_w_down, final_norm_g):
    b, t, dm = x.shape
    depth = ada_w.shape[0]
    hd = N_HEADS * HEAD_DIM
    slopes = 2.0 ** (-8.0 * jnp.arange(1, N_HEADS + 1, dtype=F32) / N_HEADS)

    attn_w_in, attn_w_out, ssm_w_in, ssm_w_gate, ssm_w_out, ffn_w_up, ffn_w_down = (
        w.astype(BF16) for w in (attn_w_in, attn_w_out, ssm_w_in, ssm_w_gate, ssm_w_out,
                                 ffn_w_up, ffn_w_down))

    mod = _ada(c, ada_w, ada_b).reshape(depth, b, 6, 1, dm)
    for i in range(depth):
        j = i // 2
        sh1, sc1, g1, sh2, sc2, g2 = (mod[i, :, q] for q in range(6))
        n1 = norm1_g[i].reshape(1, dm)
        if i % 2 == 0:
            groups = []
            for gi, (_, d) in enumerate(DILATED_GROUPS):
                if d == 1:
                    qkv, v_shared = _attn_proj(x, n1, sc1, sh1, (attn_w_in, j), gi, d)
                    groups.append(((qkv, 0), (qkv, N_HEADS), (qkv, 2 * N_HEADS)))
                else:
                    qk = _attn_proj(x, n1, sc1, sh1, (attn_w_in, j), gi, d)
                    groups.append(((qk, 0), (qk, N_HEADS), (v_shared[gi - 1], 0)))
            o = _attention(groups, slopes)
            x = _mm_res(o.reshape(b * t, hd), (attn_w_out, j), x, g1, bm=512, bk=hd)
        else:
            x = _s5_mixer(x, n1, sc1, sh1, g1, (ssm_w_in, j), ssm_log_step[j],
                          ssm_a_re[j], ssm_a_im[j], ssm_b_re[j], ssm_b_im[j], ssm_c_re[j],
                          ssm_c_im[j], ssm_d[j], (ssm_w_gate, j), (ssm_w_out, j))
        a = _ffn_up(x, norm2_g[i].reshape(1, dm), sc2, sh2, (ffn_w_up, i),
                    ffn_conv_w[i], ffn_conv_b[i].reshape(1, 2 * D_FF))
        x = _mm_res(a.reshape(b * t, D_FF), (ffn_w_down, i), x, g2, bm=512, bk=D_FF, bn=dm // 2)
    return _final_norm(x, final_norm_g.reshape(1, dm))
```

```python
import functools
import math

import jax
import jax.numpy as jnp
from jax import lax
from jax.experimental import pallas as pl
from jax.experimental.pallas import tpu as pltpu

F32 = jnp.float32
BF16 = jnp.bfloat16

D_MODEL = 2048
N_HEADS = 16
HEAD_DIM = 128
DILATED_GROUPS = ((128, 1), (512, 4), (2048, 16))
ATTN_BLOCK = 128
SSM_GROUP = 16
SSM_STATE = 64
D_FF = 5632
CONV_W = 3
EPS = 1e-6

V7X_LANES = 128
V7X_BF16_SUBLANES = 16
V7X_MXU_COLS = 256
SAFE_STRIDE = 4
V7X_SCOPED_VMEM_CAP_BYTES = 60000 * 1024
MIB = 1024 * 1024

SSM_CHUNK = 256
SSM_CHUNK_GROUPS = SSM_CHUNK // SSM_GROUP
SSM_CHUNK_STATES = SSM_CHUNK_GROUPS * SSM_STATE
SSM_SCAN_LANES = 512
LHS_K_CHUNKS = (512, 1536)
CONV_HALO = 16


def _params(semantics, vmem_bytes):
    return pltpu.CompilerParams(
        dimension_semantics=semantics,
        vmem_limit_bytes=int(min(vmem_bytes, V7X_SCOPED_VMEM_CAP_BYTES)))


def _sigmoid(x):
    return 1.0 / (1.0 + jnp.exp(-x))


def _norm_mod(x, g, sc, sh):
    ms = jnp.mean(x * x, axis=-1, keepdims=True)
    y = x * lax.rsqrt(ms + EPS)
    return (y * g) * (1.0 + sc) + sh


def _ada_kernel(c_ref, w_ref, b_ref, o_ref):
    c = c_ref[...]
    cs = (c * _sigmoid(c)).astype(BF16)
    o_ref[0] = jnp.dot(cs, w_ref[0].astype(BF16), preferred_element_type=F32) + b_ref[0]


def _ada(c, ada_w, ada_b, bn=1024):
    depth, d, n = ada_w.shape
    b = c.shape[0]
    vmem = 2 * (d * bn * 4) + d * bn * 2 + 4 * b * (d + bn) * 4 + 4 * MIB
    return pl.pallas_call(
        _ada_kernel,
        grid=(depth, n // bn),
        in_specs=[pl.BlockSpec((b, d), lambda l, j: (0, 0)),
                  pl.BlockSpec((1, d, bn), lambda l, j: (l, 0, j)),
                  pl.BlockSpec((1, 1, bn), lambda l, j: (l, 0, j))],
        out_specs=pl.BlockSpec((1, b, bn), lambda l, j: (l, 0, j)),
        out_shape=jax.ShapeDtypeStruct((depth, b, n), F32),
        compiler_params=_params(("parallel", "parallel"), vmem),
        name="ada_mod",
    )(c, ada_w, ada_b.reshape(depth, 1, n))


def _gather_classes(stage_ref, stage2_ref, chunk, nrows, dils, emit):
    quarter = nrows // SAFE_STRIDE
    for r0 in range(SAFE_STRIDE):
        rows = stage_ref[chunk, pl.ds(r0, quarter, stride=SAFE_STRIDE), :]
        if SAFE_STRIDE in dils:
            emit(SAFE_STRIDE, r0, rows)
        if any(d > SAFE_STRIDE for d in dils):
            stage2_ref[chunk, pl.ds(r0 * quarter, quarter), :] = rows
    for d in dils:
        if d > SAFE_STRIDE:
            step = d // SAFE_STRIDE
            assert step <= SAFE_STRIDE
            for r0 in range(SAFE_STRIDE):
                for r1 in range(step):
                    emit(d, SAFE_STRIDE * r1 + r0,
                         stage2_ref[chunk, pl.ds(r0 * quarter + r1, nrows // d, stride=step), :])


def _scatter_class(stage2_ref, chunk, nrows, d, r, rows):
    step = d // SAFE_STRIDE
    assert SAFE_STRIDE < d and step <= SAFE_STRIDE
    r0, r1 = r % SAFE_STRIDE, r // SAFE_STRIDE
    stage2_ref[chunk, pl.ds(r0 * (nrows // SAFE_STRIDE) + r1, nrows // d, stride=step), :] = rows


def _scatter_finish(stage_ref, stage2_ref, chunk, nrows):
    quarter = nrows // SAFE_STRIDE
    for r0 in range(SAFE_STRIDE):
        stage_ref[chunk, pl.ds(r0, quarter, stride=SAFE_STRIDE), :] = (
            stage2_ref[chunk, pl.ds(r0 * quarter, quarter), :])


def _attn_proj_kernel(x_ref, g_ref, sc_ref, sh_ref, w_ref, o_ref, *rest, d, bm, bn, rc, kc, v_dils,
                      v_first_block):
    nv = len(v_dils)
    v_refs, lhs_ref, inv_ref, stage_ref = rest[:nv], rest[nv], rest[nv + 1], rest[nv + 2]
    stage2_ref = rest[nv + 3] if len(rest) > nv + 3 else None
    j = pl.program_id(2)
    rpc = bm // d
    dm = x_ref.shape[-1]
    heads = bn // HEAD_DIM

    def write_out(acc):
        for hh in range(heads):
            for r in range(d):
                o_ref[0, hh, r] = acc[r * rpc:(r + 1) * rpc,
                                      hh * HEAD_DIM:(hh + 1) * HEAD_DIM].astype(BF16)

    @pl.when(j == 0)
    def _():
        g, sc, sh = g_ref[...], sc_ref[0], sh_ref[0]
        for c in range(bm // rc):
            rows = pl.ds(c * rc, rc)
            x = x_ref[0, rows, :]
            inv = lax.rsqrt(jnp.mean(x * x, axis=-1, keepdims=True) + EPS)
            inv_ref[rows, :] = jnp.broadcast_to(inv, (rc, V7X_LANES))
        acc = None
        assert sum(kc) == dm
        for k0, kw in zip((sum(kc[:n]) for n in range(len(kc))), kc):
            for lc in range(k0 // V7X_LANES, (k0 + kw) // V7X_LANES):
                lanes = pl.ds(lc * V7X_LANES, V7X_LANES)
                lo, hi = lc * V7X_LANES, (lc + 1) * V7X_LANES
                for c in range(bm // rc):
                    rows = pl.ds(c * rc, rc)
                    h = (((x_ref[0, rows, lanes] * inv_ref[rows, :]) * g[:, lo:hi])
                         * (1.0 + sc[:, lo:hi]) + sh[:, lo:hi])
                    if d == 1:
                        lhs_ref[rows, lanes] = h.astype(BF16)
                    else:
                        stage_ref[lc, rows, :] = h
                if d > 1:
                    def to_lhs(_, r, rows, lanes=lanes):
                        lhs_ref[pl.ds(r * rpc, rpc), lanes] = rows.astype(BF16)
                    _gather_classes(stage_ref, stage2_ref, lc, bm, (d,), to_lhs)
            klanes = pl.ds(k0, kw)
            part = jnp.dot(lhs_ref[:, klanes], w_ref[klanes, :], preferred_element_type=F32)
            acc = part if acc is None else acc + part
        write_out(acc)

    @pl.when(j > 0)
    def _():
        acc = jnp.dot(lhs_ref[...], w_ref[...], preferred_element_type=F32)
        write_out(acc)
        if v_refs:
            @pl.when(j >= v_first_block)
            def _():
                for hh in range(heads):
                    stage_ref[hh] = acc[:, hh * HEAD_DIM:(hh + 1) * HEAD_DIM]
                for hh in range(heads):
                    def to_v(dd, r, rows, hh=hh):
                        v_refs[v_dils.index(dd)][0, hh, r] = rows.astype(BF16)
                    _gather_classes(stage_ref, stage2_ref, hh, bm, v_dils, to_v)


def _attn_proj(x, g, sc, sh, w, gi, d, bm=1024, bn=1024):
    b, t, dm = x.shape
    w, layer = w
    hd = N_HEADS * HEAD_DIM
    parts = 3 if d == 1 else 2
    n = parts * hd
    per_part = hd // bn
    n_groups = len(DILATED_GROUPS)
    part_start = (gi * per_part, (n_groups + gi) * per_part, 2 * n_groups * per_part)
    v_dils = tuple(dd for _, dd in DILATED_GROUPS if dd > 1) if d == 1 else ()
    v_first_block = 2 * per_part

    def w_block(bi, i, j):
        part = j // per_part
        start = jnp.where(part == 0, part_start[0],
                          jnp.where(part == 1, part_start[1], part_start[2]))
        return (layer, 0, start + j % per_part)

    rpc = bm // d
    rc = 256
    heads = bn // HEAD_DIM
    assert rpc % V7X_BF16_SUBLANES == 0 and bm % rc == 0 and t % bm == 0 and hd % bn == 0
    assert all((bm // dd) % V7X_BF16_SUBLANES == 0 for dd in v_dils)
    stage_chunks = heads if d == 1 else dm // V7X_LANES
    n_stages = 2 if max(v_dils + (d,)) > SAFE_STRIDE else 1
    scratch = ([pltpu.VMEM((bm, dm), BF16), pltpu.VMEM((bm, V7X_LANES), F32)]
               + [pltpu.VMEM((stage_chunks, bm, V7X_LANES), F32)] * n_stages)
    out_specs = [pl.BlockSpec((1, heads, d, rpc, HEAD_DIM), lambda bi, i, j: (bi, j, 0, i, 0))]
    out_shape = [jax.ShapeDtypeStruct((b, n // HEAD_DIM, d, t // d, HEAD_DIM), BF16)]
    for dd in v_dils:
        out_specs.append(pl.BlockSpec(
            (1, heads, dd, bm // dd, HEAD_DIM),
            lambda bi, i, j: (bi, jnp.maximum(j - v_first_block, 0), 0, i, 0)))
        out_shape.append(jax.ShapeDtypeStruct((b, N_HEADS, dd, t // dd, HEAD_DIM), BF16))
    vmem = (2 * bm * dm * 4 + bm * dm * 2 + n_stages * stage_chunks * bm * V7X_LANES * 4
            + 2 * dm * bn * 2
            + 2 * (1 + len(v_dils)) * bm * bn * 2 + bm * bn * 4 + 4 * rc * dm * 4 + 4 * MIB)
    outs = pl.pallas_call(
        functools.partial(_attn_proj_kernel, d=d, bm=bm, bn=bn, rc=rc, kc=LHS_K_CHUNKS, v_dils=v_dils,
                          v_first_block=v_first_block),
        grid=(b, t // bm, n // bn),
        in_specs=[pl.BlockSpec((1, bm, dm), lambda bi, i, j: (bi, i, 0)),
                  pl.BlockSpec((1, dm), lambda bi, i, j: (0, 0)),
                  pl.BlockSpec((1, 1, dm), lambda bi, i, j: (bi, 0, 0)),
                  pl.BlockSpec((1, 1, dm), lambda bi, i, j: (bi, 0, 0)),
                  pl.BlockSpec((None, dm, bn), w_block)],
        out_specs=out_specs,
        out_shape=out_shape,
        scratch_shapes=scratch,
        compiler_params=_params(("parallel", "parallel", "arbitrary"), vmem),
        name=f"attn_proj_d{d}",
    )(x, g, sc, sh, w)
    outs = [o.reshape(b, o.shape[1], t, HEAD_DIM) for o in outs]
    return (outs[0], outs[1:]) if d == 1 else outs[0]


def _attn_kernel(slopes_ref, q1, k1, v1, q4, k4, v4, q16, k16, v16, o_ref, o_scr, l_scr, *, t,
                 stage_batch):
    blk = ATTN_BLOCK
    slope = slopes_ref[pl.program_id(1)]
    scale = HEAD_DIM ** -0.5
    qi = lax.broadcasted_iota(jnp.int32, (blk, 2 * blk), 0)
    kj = lax.broadcasted_iota(jnp.int32, (blk, 2 * blk), 1)
    dist = blk + qi - kj
    in_band = (dist >= 0) & (dist <= blk)

    refs = ((q1, k1, v1), (q4, k4, v4), (q16, k16, v16))
    stage_slot = len(DILATED_GROUPS)
    quarter = t // SAFE_STRIDE
    assert sum(d > SAFE_STRIDE for _, d in DILATED_GROUPS) <= 1
    work = []
    for gi, ((_, d), (q_ref, k_ref, v_ref)) in enumerate(zip(DILATED_GROUPS, refs)):
        cls_len = t // d
        nb = cls_len // blk
        bias_full = jnp.where(in_band, -slope * (dist * d).astype(F32), -jnp.inf)
        bias_first = bias_full[:, blk:]

        for r in range(d):
            base = r * cls_len
            for n in range(nb):
                if d == 1:
                    dst = (gi, pl.ds(n * blk, blk))
                elif d <= SAFE_STRIDE:
                    dst = (gi, pl.ds(n * blk * d + r, blk, stride=d))
                else:
                    step = d // SAFE_STRIDE
                    r0, r1 = r % SAFE_STRIDE, r // SAFE_STRIDE
                    dst = (stage_slot, pl.ds(r0 * quarter + step * n * blk + r1, blk, stride=step))
                q_rows = pl.ds(base + n * blk, blk)
                if n == 0:
                    work.append((q_ref, k_ref, v_ref, q_rows, pl.ds(base, blk), bias_first, dst))
                else:
                    work.append((q_ref, k_ref, v_ref, q_rows,
                                 pl.ds(base + (n - 1) * blk, 2 * blk), bias_full, dst))

    for w0 in range(0, len(work), stage_batch):
        batch = work[w0:w0 + stage_batch]
        ss = [lax.dot_general(q_ref[0, 0, qr, :], k_ref[0, 0, kr, :], (((1,), (1,)), ((), ())),
                              preferred_element_type=F32) * scale + bias
              for (q_ref, k_ref, _, qr, kr, bias, _) in batch]
        ms = [jnp.max(s, axis=-1, keepdims=True) for s in ss]
        es = [jnp.exp(s - m).astype(BF16) for s, m in zip(ss, ms)]
        outs = []
        for e, (_, _, v_ref, _, kr, _, _) in zip(es, batch):
            vv = v_ref[0, 0, kr, :]
            outs.append(jnp.dot(e, jnp.concatenate([vv, jnp.ones_like(vv)], axis=1),
                                preferred_element_type=F32))
        for o, m, (_, _, _, _, _, _, (slot, rows)) in zip(outs, ms, batch):
            den = o[:, HEAD_DIM:]
            o_scr[slot, rows, :] = o[:, :HEAD_DIM] / den
            l_scr[slot, rows, :] = m + jnp.log(den)

    for gi, (_, d) in enumerate(DILATED_GROUPS):
        if d > SAFE_STRIDE:
            for scr in (o_scr, l_scr):
                for r0 in range(SAFE_STRIDE):
                    for c in range(quarter // blk):
                        scr[gi, pl.ds(r0 + c * blk * SAFE_STRIDE, blk, stride=SAFE_STRIDE), :] = (
                            scr[stage_slot, pl.ds(r0 * quarter + c * blk, blk), :])

    rc = 256

    def combine(c, carry):
        rows = pl.ds(pl.multiple_of(c * rc, rc), rc)
        l0, l1, l2 = l_scr[0, rows, :], l_scr[1, rows, :], l_scr[2, rows, :]
        mx = jnp.maximum(jnp.maximum(l0, l1), l2)
        w0, w1, w2 = jnp.exp(l0 - mx), jnp.exp(l1 - mx), jnp.exp(l2 - mx)
        o = (w0 * o_scr[0, rows, :] + w1 * o_scr[1, rows, :] + w2 * o_scr[2, rows, :]) \
            / (w0 + w1 + w2)
        o_ref[0, rows, :] = o.astype(BF16)
        return carry

    lax.fori_loop(0, t // rc, combine, 0, unroll=2)


def _attention(groups, slopes, stage_batch=8):
    b, _, t, dh = groups[0][0][0].shape
    h = N_HEADS
    in_specs = [pl.BlockSpec(memory_space=pltpu.SMEM)]
    args = [slopes]
    for group in groups:
        for arr, head0 in group:
            in_specs.append(pl.BlockSpec(
                (1, 1, t, dh), functools.partial(lambda bi, hi, head0: (bi, head0 + hi, 0, 0),
                                                 head0=head0)))
            args.append(arr)
    vmem = 2 * 9 * t * dh * 2 + 2 * t * dh * 2 + 8 * t * dh * 4 + 8 * MIB
    return pl.pallas_call(
        functools.partial(_attn_kernel, t=t, stage_batch=stage_batch),
        grid=(b, h),
        in_specs=in_specs,
        out_specs=pl.BlockSpec((1, t, dh), lambda bi, hi: (bi, 0, hi)),
        out_shape=jax.ShapeDtypeStruct((b, t, h * dh), BF16),
        scratch_shapes=[pltpu.VMEM((len(groups) + 1, t, dh), F32)] * 2,
        compiler_params=_params(("parallel", "parallel"), vmem),
        name="dilated_attn",
    )(*args)


def _mm_res_kernel(lhs_ref, w_ref, x_ref, gate_ref, o_ref, *scratch, nk, tb_rows, rc):
    acc_ref = accs_ref = None
    if tb_rows is not None:
        acc_ref, accs_ref, accs2_ref = scratch
    elif nk > 1:
        acc_ref, = scratch
    k = pl.program_id(2)
    part = jnp.dot(lhs_ref[...], w_ref[...], preferred_element_type=F32)

    def finish(xs, gate, res):
        return xs + gate * res

    if acc_ref is None:
        o_ref[...] = finish(x_ref[...], gate_ref[0], part)
        return

    if nk == 1:
        acc_ref[...] = part
    else:
        @pl.when(k == 0)
        def _():
            acc_ref[...] = part

        @pl.when(k > 0)
        def _():
            acc_ref[...] += part

    @pl.when(k == nk - 1)
    def _():
        if tb_rows is None:
            gate = gate_ref[0]
            bm = acc_ref.shape[0]
            for c in range(bm // rc):
                rows = pl.ds(c * rc, rc)
                o_ref[rows, :] = finish(x_ref[rows, :], gate, acc_ref[rows, :])
        else:
            nb, bt = tb_rows
            nlc = acc_ref.shape[-1] // V7X_LANES
            for lc in range(nlc):
                lanes = pl.ds(lc * V7X_LANES, V7X_LANES)
                accs_ref[lc] = acc_ref[:, lanes]

                def put(_, bi, res, lanes=lanes):
                    o_ref[bi, :, lanes] = finish(x_ref[bi, :, lanes], gate_ref[bi, :, lanes], res)

                _gather_classes(accs_ref, accs2_ref, lc, nb * bt, (nb,), put)


def _mm_res(lhs, w, x, gate, *, bm, bk, bn=None, time_batch=False):
    b, t, dm = x.shape
    w, layer = w
    m, kdim = lhs.shape
    bn = dm if bn is None else bn
    assert m == b * t and kdim % bk == 0 and m % bm == 0 and dm % bn == 0
    nk = kdim // bk
    if time_batch:
        assert bn == dm and b > SAFE_STRIDE
        bt = bm // b
        x_in, x_spec = x, pl.BlockSpec((b, bt, dm), lambda j, i, k: (0, i, 0))
        gate_spec = pl.BlockSpec((b, 1, dm), lambda j, i, k: (0, 0, 0))
        out_shape = jax.ShapeDtypeStruct((b, t, dm), F32)
        tb_rows = (b, bt)
        scratch = ([pltpu.VMEM((bm, dm), F32)]
                   + [pltpu.VMEM((dm // V7X_LANES, bm, V7X_LANES), F32)] * 2)
    else:
        assert t % bm == 0
        x_in, x_spec = x.reshape(m, dm), pl.BlockSpec((bm, bn), lambda j, i, k: (i, j))
        gate_spec = pl.BlockSpec((1, 1, bn), lambda j, i, k: (i * bm // t, 0, j))
        out_shape = jax.ShapeDtypeStruct((m, dm), F32)
        tb_rows = None
        scratch = [pltpu.VMEM((bm, bn), F32)] if nk > 1 else []
    in_specs = [pl.BlockSpec((bm, bk), lambda j, i, k: (i, k)),
                pl.BlockSpec((None, bk, bn), lambda j, i, k: (layer, k, j)),
                x_spec, gate_spec]
    args = [lhs, w, x_in, gate]
    rc = min(bm, 256)
    vmem = (2 * bm * bk * 2 + 2 * bk * bn * 2 + 4 * bm * bn * 4 + (2 + 2 * time_batch) * bm * bn * 4
            + 6 * rc * bn * 4 + 4 * MIB)
    out = pl.pallas_call(
        functools.partial(_mm_res_kernel, nk=nk, tb_rows=tb_rows, rc=rc),
        grid=(dm // bn, m // bm, nk),
        in_specs=in_specs,
        out_specs=x_spec,
        out_shape=out_shape,
        scratch_shapes=scratch,
        compiler_params=_params(("parallel", "parallel", "arbitrary"), vmem),
        name="mm_res_tb" if time_batch else "mm_res",
    )(*args)
    return out.reshape(b, t, dm)


def _final_norm_kernel(x_ref, g_ref, o_ref):
    x = x_ref[...]
    ms = jnp.mean(x * x, axis=-1, keepdims=True)
    o_ref[...] = (x * lax.rsqrt(ms + EPS)) * g_ref[...]


def _final_norm(x, g, bm=1024):
    b, t, dm = x.shape
    m = b * t
    out = pl.pallas_call(
        _final_norm_kernel,
        grid=(m // bm,),
        in_specs=[pl.BlockSpec((bm, dm), lambda i: (i, 0)),
                  pl.BlockSpec((1, dm), lambda i: (0, 0))],
        out_specs=pl.BlockSpec((bm, dm), lambda i: (i, 0)),
        out_shape=jax.ShapeDtypeStruct((m, dm), F32),
        compiler_params=_params(("parallel",), 8 * bm * dm * 4 + 4 * MIB),
        name="final_norm",
    )(x.reshape(m, dm), g)
    return out.reshape(b, t, dm)


def _ffn_up_kernel(x_ref, xh_ref, g_ref, sc_ref, sh_ref, wg_ref, wu_ref, cwg_ref, cwu_ref,
                   cbg_ref, cbu_ref, o_ref, lhs_ref, inv_ref, *, bm, rc, kc):
    i, j = pl.program_id(1), pl.program_id(2)
    dm = x_ref.shape[-1]
    bn = o_ref.shape[-1]
    up_cols = [pl.ds(s * V7X_MXU_COLS, V7X_MXU_COLS) for s in range(bn // V7X_MXU_COLS)]

    def conv(a, cw_ref, cb_ref, cols):
        cw = cw_ref[:, cols]
        taps = pltpu.roll(cw[0:1] * a, 1, axis=0) + cw[1:2] * a
        taps = pltpu.roll(taps, 1, axis=0) + cw[2:3] * a
        return cb_ref[:, cols] + taps[CONV_HALO:]

    def glu(act, a_up, s):
        up = conv(a_up, cwu_ref, cbu_ref, up_cols[s])
        o_ref[0, :, up_cols[s]] = (act[:, s * V7X_MXU_COLS:(s + 1) * V7X_MXU_COLS] * up).astype(BF16)

    @pl.when(j == 0)
    def _():
        g, sc, sh = g_ref[...], sc_ref[0], sh_ref[0]

        def inv_rms(x):
            return jnp.broadcast_to(lax.rsqrt(jnp.mean(x * x, axis=-1, keepdims=True) + EPS),
                                    (x.shape[0], V7X_LANES))

        inv_ref[pl.ds(0, CONV_HALO), :] = inv_rms(xh_ref[0])
        for c in range(bm // rc):
            inv_ref[pl.ds(CONV_HALO + c * rc, rc), :] = inv_rms(x_ref[0, pl.ds(c * rc, rc), :])

        a_g, a_u = None, [None] * len(up_cols)
        assert sum(kc) == dm
        for k0, kw in zip((sum(kc[:n]) for n in range(len(kc))), kc):
            for lc in range(k0 // V7X_LANES, (k0 + kw) // V7X_LANES):
                lanes = pl.ds(lc * V7X_LANES, V7X_LANES)
                lo, hi = lc * V7X_LANES, (lc + 1) * V7X_LANES

                def norm_mod(x, inv):
                    return ((x * inv) * g[:, lo:hi]) * (1.0 + sc[:, lo:hi]) + sh[:, lo:hi]

                halo = norm_mod(xh_ref[0, :, lanes], inv_ref[pl.ds(0, CONV_HALO), :])
                lhs_ref[pl.ds(0, CONV_HALO), lanes] = jnp.where(i == 0, 0.0, halo).astype(BF16)
                for c in range(bm // rc):
                    rows = pl.ds(CONV_HALO + c * rc, rc)
                    lhs_ref[rows, lanes] = norm_mod(x_ref[0, pl.ds(c * rc, rc), lanes],
                                                    inv_ref[rows, :]).astype(BF16)
            klanes = pl.ds(k0, kw)
            lk = lhs_ref[:, klanes]
            part = jnp.dot(lk, wg_ref[klanes, :], preferred_element_type=F32)
            a_g = part if a_g is None else a_g + part
            for s, cols in enumerate(up_cols):
                part = jnp.dot(lk, wu_ref[klanes, cols], preferred_element_type=F32)
                a_u[s] = part if a_u[s] is None else a_u[s] + part
        gate = conv(a_g, cwg_ref, cbg_ref, pl.ds(0, bn))
        act = gate * _sigmoid(gate)
        for s in range(len(up_cols)):
            glu(act, a_u[s], s)

    @pl.when(j > 0)
    def _():
        lhs = lhs_ref[...]
        gate = conv(jnp.dot(lhs, wg_ref[...], preferred_element_type=F32), cwg_ref, cbg_ref,
                    pl.ds(0, bn))
        act = gate * _sigmoid(gate)
        for s, cols in enumerate(up_cols):
            glu(act, jnp.dot(lhs, wu_ref[:, cols], preferred_element_type=F32), s)


def _ffn_up(x, g, sc, sh, w_up, conv_w, conv_b, bm=1024, bn=512):
    b, t, dm = x.shape
    w_up, layer = w_up
    f = w_up.shape[-1] // 2
    assert t % bm == 0 and f % bn == 0 and bm % CONV_HALO == 0
    nj = f // bn
    rc = 256
    vmem = (2 * bm * dm * 4 + (bm + CONV_HALO) * dm * 2 + 4 * dm * bn * 2 + 2 * bm * bn * 2
            + 8 * (bm + CONV_HALO) * bn * 4 + 4 * rc * dm * 4 + 4 * MIB)
    halo_blocks = bm // CONV_HALO
    return pl.pallas_call(
        functools.partial(_ffn_up_kernel, bm=bm, rc=rc, kc=LHS_K_CHUNKS),
        grid=(b, t // bm, nj),
        in_specs=[pl.BlockSpec((1, bm, dm), lambda bi, i, j: (bi, i, 0)),
                  pl.BlockSpec((1, CONV_HALO, dm),
                               lambda bi, i, j: (bi, jnp.maximum(i * halo_blocks - 1, 0), 0)),
                  pl.BlockSpec((1, dm), lambda bi, i, j: (0, 0)),
                  pl.BlockSpec((1, 1, dm), lambda bi, i, j: (bi, 0, 0)),
                  pl.BlockSpec((1, 1, dm), lambda bi, i, j: (bi, 0, 0)),
                  pl.BlockSpec((None, dm, bn), lambda bi, i, j: (layer, 0, j)),
                  pl.BlockSpec((None, dm, bn), lambda bi, i, j: (layer, 0, nj + j)),
                  pl.BlockSpec((CONV_W, bn), lambda bi, i, j: (0, j)),
                  pl.BlockSpec((CONV_W, bn), lambda bi, i, j: (0, nj + j)),
                  pl.BlockSpec((1, bn), lambda bi, i, j: (0, j)),
                  pl.BlockSpec((1, bn), lambda bi, i, j: (0, nj + j))],
        out_specs=pl.BlockSpec((1, bm, bn), lambda bi, i, j: (bi, i, j)),
        out_shape=jax.ShapeDtypeStruct((b, t, f), BF16),
        scratch_shapes=[pltpu.VMEM((bm + CONV_HALO, dm), BF16),
                        pltpu.VMEM((bm + CONV_HALO, V7X_LANES), F32)],
        compiler_params=_params(("parallel", "parallel", "arbitrary"), vmem),
        name="ffn_up",
    )(x, x, g, sc, sh, w_up, w_up, conv_w, conv_w, conv_b, conv_b)


def _ssm_disc_kernel(ls_ref, ar_ref, ai_ref, br_ref, bi_ref, lr_ref, li_ref, bbr_ref, bbi_ref):
    step = jnp.exp(ls_ref[...])
    lr, li = ar_ref[...], ai_ref[...]
    mag = jnp.exp(lr * step)
    ang = li * step
    lb_re = mag * jnp.cos(ang)
    lb_im = mag * jnp.sin(ang)
    nr, ni = lb_re - 1.0, lb_im
    den = lr * lr + li * li
    coef_re = (nr * lr + ni * li) / den
    coef_im = (ni * lr - nr * li) / den
    br, bi = br_ref[...], bi_ref[...]
    lr_ref[...] = lb_re
    li_ref[...] = lb_im
    bbr_ref[...] = coef_re * br - coef_im * bi
    bbi_ref[...] = coef_re * bi + coef_im * br


def _ssm_discretize(log_step, a_re, a_im, b_re, b_im, bg=SSM_CHUNK_GROUPS):
    g, p, c = b_re.shape
    col = jax.ShapeDtypeStruct((g, p, 1), F32)
    full = jax.ShapeDtypeStruct((g, p, c), F32)
    col_spec = pl.BlockSpec((bg, p, 1), lambda i: (i, 0, 0))
    full_spec = pl.BlockSpec((bg, p, c), lambda i: (i, 0, 0))
    return pl.pallas_call(
        _ssm_disc_kernel,
        grid=(g // bg,),
        in_specs=[pl.BlockSpec((bg, 1, 1), lambda i: (i, 0, 0)), col_spec, col_spec,
                  full_spec, full_spec],
        out_specs=(col_spec, col_spec, full_spec, full_spec),
        out_shape=(col, col, full, full),
        compiler_params=_params(("parallel",), 32 * MIB),
        name="ssm_discretize",
    )(log_step.reshape(g, 1, 1), a_re.reshape(g, p, 1), a_im.reshape(g, p, 1), b_re, b_im)


def _ssm_in_kernel(x_ref, g_ref, sc_ref, sh_ref, w_ref, o_ref, hs_ref, hs2_ref, lhs_ref, *, nb, bt):
    g = g_ref[...]
    nlc = x_ref.shape[-1] // V7X_LANES
    for bi in range(nb):
        h = _norm_mod(x_ref[bi], g, sc_ref[bi], sh_ref[bi])
        for lc in range(nlc):
            _scatter_class(hs2_ref, lc, nb * bt, nb, bi, h[:, lc * V7X_LANES:(lc + 1) * V7X_LANES])
    for lc in range(nlc):
        _scatter_finish(hs_ref, hs2_ref, lc, nb * bt)
        lhs_ref[:, pl.ds(lc * V7X_LANES, V7X_LANES)] = hs_ref[lc].astype(BF16)
    o_ref[...] = jnp.dot(lhs_ref[...], w_ref[...], preferred_element_type=F32)


def _ssm_in(x, g, sc, sh, w, bt=32):
    b, t, dm = x.shape
    w, layer = w
    n = w.shape[-1]
    bm = bt * b
    vmem = (2 * bm * dm * 4 + 2 * bm * dm * 4 + bm * dm * 2 + 2 * dm * n * 2 + 2 * bm * n * 4
            + bm * n * 4 + 4 * MIB)
    return pl.pallas_call(
        functools.partial(_ssm_in_kernel, nb=b, bt=bt),
        grid=(t // bt,),
        in_specs=[pl.BlockSpec((b, bt, dm), lambda i: (0, i, 0)),
                  pl.BlockSpec((1, dm), lambda i: (0, 0)),
                  pl.BlockSpec((b, 1, dm), lambda i: (0, 0, 0)),
                  pl.BlockSpec((b, 1, dm), lambda i: (0, 0, 0)),
                  pl.BlockSpec((None, dm, n), lambda i: (layer, 0, 0))],
        out_specs=pl.BlockSpec((bm, n), lambda i: (i, 0)),
        out_shape=jax.ShapeDtypeStruct((t * b, n), F32),
        scratch_shapes=[pltpu.VMEM((dm // V7X_LANES, bm, V7X_LANES), F32),
                        pltpu.VMEM((dm // V7X_LANES, bm, V7X_LANES), F32),
                        pltpu.VMEM((bm, dm), BF16)],
        compiler_params=_params(("parallel",), vmem),
        name="ssm_in_proj",
    )(x, g, sc, sh, w)


def _gelu_tanh(y):
    return 0.5 * y * (1.0 + jnp.tanh(math.sqrt(2.0 / math.pi) * (y + 0.044715 * (y * y * y))))


def _ssm_core_kernel(u_ref, bmat_ref, lr_ref, li_ref, cre_ref, cim_ref, dsk_ref, o_ref,
                     st_ref, buf_ref, *, nb, rb):
    s = SSM_CHUNK_STATES
    hw = SSM_SCAN_LANES
    nhalf = s // hw

    @pl.when(pl.program_id(1) == 0)
    def _():
        st_ref[...] = jnp.zeros_like(st_ref)

    ub = u_ref[...].astype(BF16)

    def state_cols(half):
        return pl.ds(half * hw, hw), pl.ds(s + half * hw, hw)

    def b_proj(half):
        for cols in state_cols(half):
            buf_ref[:, cols] = jnp.dot(ub, bmat_ref[0, :, cols], preferred_element_type=F32)

    def scan(half):
        cr, ci = state_cols(half)
        lr = jnp.broadcast_to(lr_ref[0, :, cr], (nb, hw))
        li = jnp.broadcast_to(li_ref[0, :, cr], (nb, hw))
        xr, xi = st_ref[:, cr], st_ref[:, ci]
        for tt in range(rb // nb):
            rows = pl.ds(tt * nb, nb)
            xr, xi = (lr * xr - li * xi + buf_ref[rows, cr], lr * xi + li * xr + buf_ref[rows, ci])
            buf_ref[rows, cr] = xr
            buf_ref[rows, ci] = xi
        st_ref[:, cr] = xr
        st_ref[:, ci] = xi

    def c_proj(half):
        cr, ci = state_cols(half)
        srows = pl.ds(half * hw, hw)
        return (jnp.dot(buf_ref[:, cr].astype(BF16), cre_ref[0, srows, :], preferred_element_type=F32)
                - jnp.dot(buf_ref[:, ci].astype(BF16), cim_ref[0, srows, :],
                          preferred_element_type=F32))

    b_proj(0)
    y = None
    for half in range(nhalf):
        if half + 1 < nhalf:
            b_proj(half + 1)
        scan(half)
        part = c_proj(half)
        y = part if y is None else y + part

    y = y + dsk_ref[0] * u_ref[...]
    o_ref[...] = _gelu_tanh(y).astype(BF16)


def _ssm_core(u, bmat, lam_re, lam_im, cre, cim, d_skip, nb, rb=2048):
    m, dm = u.shape
    nchunk = dm // SSM_CHUNK
    s = SSM_CHUNK_STATES
    vmem = (2 * rb * SSM_CHUNK * 4 + 2 * SSM_CHUNK * 2 * s * 2 + 4 * s * SSM_CHUNK * 2
            + 2 * rb * SSM_CHUNK * 2 + rb * 2 * s * 4 + nb * 2 * s * 4
            + 3 * rb * SSM_SCAN_LANES * 4 + 4 * MIB)
    return pl.pallas_call(
        functools.partial(_ssm_core_kernel, nb=nb, rb=rb),
        grid=(nchunk, m // rb),
        in_specs=[pl.BlockSpec((rb, SSM_CHUNK), lambda c, i: (i, c)),
                  pl.BlockSpec((1, SSM_CHUNK, 2 * s), lambda c, i: (c, 0, 0)),
                  pl.BlockSpec((1, 1, s), lambda c, i: (c, 0, 0)),
                  pl.BlockSpec((1, 1, s), lambda c, i: (c, 0, 0)),
                  pl.BlockSpec((1, s, SSM_CHUNK), lambda c, i: (c, 0, 0)),
                  pl.BlockSpec((1, s, SSM_CHUNK), lambda c, i: (c, 0, 0)),
                  pl.BlockSpec((1, 1, SSM_CHUNK), lambda c, i: (c, 0, 0))],
        out_specs=pl.BlockSpec((rb, SSM_CHUNK), lambda c, i: (i, c)),
        out_shape=jax.ShapeDtypeStruct((m, dm), BF16),
        scratch_shapes=[pltpu.VMEM((nb, 2 * s), F32), pltpu.VMEM((rb, 2 * s), F32)],
        compiler_params=_params(("parallel", "arbitrary"), vmem),
        name="ssm_core",
    )(u, bmat, lam_re, lam_im, cre, cim, d_skip)


def _ssm_gate_kernel(g_ref, w_ref, o_ref):
    g = g_ref[...]
    z = jnp.dot(g, w_ref[...], preferred_element_type=F32)
    o_ref[...] = (g.astype(F32) * _sigmoid(z)).astype(BF16)


def _ssm_gate(g, w, bm=1024):
    m, dm = g.shape
    w, layer = w
    vmem = 4 * bm * dm * 2 + 2 * dm * dm * 2 + 4 * bm * dm * 4 + 4 * MIB
    return pl.pallas_call(
        _ssm_gate_kernel,
        grid=(m // bm,),
        in_specs=[pl.BlockSpec((bm, dm), lambda i: (i, 0)),
                  pl.BlockSpec((None, dm, dm), lambda i: (layer, 0, 0))],
        out_specs=pl.BlockSpec((bm, dm), lambda i: (i, 0)),
        out_shape=jax.ShapeDtypeStruct((m, dm), BF16),
        compiler_params=_params(("parallel",), vmem),
        name="ssm_glu",
    )(g, w)


def _block_diag_chunks(a):
    g, r, s = a.shape
    n = SSM_CHUNK_GROUPS
    eye = jnp.eye(n, dtype=a.dtype)
    out = a.reshape(g // n, n, r, 1, s) * eye.reshape(1, n, 1, n, 1)
    return out.reshape(g // n, n * r, n * s)


def _s5_mixer(x, g, sc, sh, gate, w_in, log_step, a_re, a_im, b_re, b_im, c_re, c_im, d_skip,
              w_gate, w_out):
    b, t, dm = x.shape
    ng = dm // SSM_GROUP
    nchunk = dm // SSM_CHUNK
    s = SSM_CHUNK_STATES
    lam_re, lam_im, bb_re, bb_im = _ssm_discretize(log_step, a_re, a_im, b_re, b_im)
    bmat = jnp.concatenate([_block_diag_chunks(bb_re.transpose(0, 2, 1)),
                            _block_diag_chunks(bb_im.transpose(0, 2, 1))], axis=-1).astype(BF16)
    cre = _block_diag_chunks(c_re.transpose(0, 2, 1)).astype(BF16)
    cim = _block_diag_chunks(c_im.transpose(0, 2, 1)).astype(BF16)
    lam_re = lam_re.reshape(nchunk, 1, s)
    lam_im = lam_im.reshape(nchunk, 1, s)
    u = _ssm_in(x, g, sc, sh, w_in)
    gq = _ssm_core(u, bmat, lam_re, lam_im, cre, cim, d_skip.reshape(nchunk, 1, SSM_CHUNK), nb=b)
    gq = _ssm_gate(gq, w_gate)
    return _mm_res(gq, w_out, x, gate, bm=32 * b, bk=dm, time_batch=True)


def kernel(x, c, ada_w, ada_b, norm1_g, norm2_g, attn_w_in, attn_w_out, ssm_w_in, ssm_log_step, ssm_a_re, ssm_a_im, ssm_b_re, ssm_b_im, ssm_c_re, ssm_c_im, ssm_d, ssm_w_gate, ssm_w_out, ffn_w_up, ffn_conv_w, ffn_conv_b, ffn_w_down, final_norm_g):
    b, t, dm = x.shape
    depth = ada_w.shape[0]
    hd = N_HEADS * HEAD_DIM
    slopes = 2.0 ** (-8.0 * jnp.arange(1, N_HEADS + 1, dtype=F32) / N_HEADS)

    attn_w_in, attn_w_out, ssm_w_in, ssm_w_gate, ssm_w_out, ffn_w_up, ffn_w_down = (
        w.astype(BF16) for w in (attn_w_in, attn_w_out, ssm_w_in, ssm_w_gate, ssm_w_out,
                                 ffn_w_up, ffn_w_down))

    mod = _ada(c, ada_w, ada_b).reshape(depth, b, 6, 1, dm)
    for i in range(depth):
        j = i // 2
        sh1, sc1, g1, sh2, sc2, g2 = (mod[i, :, q] for q in range(6))
        n1 = norm1_g[i].reshape(1, dm)
        if i % 2 == 0:
            groups = []
            for gi, (_, d) in enumerate(DILATED_GROUPS):
                if d == 1:
                    qkv, v_shared = _attn_proj(x, n1, sc1, sh1, (attn_w_in, j), gi, d)
                    groups.append(((qkv, 0), (qkv, N_HEADS), (qkv, 2 * N_HEADS)))
                else:
                    qk = _attn_proj(x, n1, sc1, sh1, (attn_w_in, j), gi, d)
                    groups.append(((qk, 0), (qk, N_HEADS), (v_shared[gi - 1], 0)))
            o = _attention(groups, slopes)
            x = _mm_res(o.reshape(b * t, hd), (attn_w_out, j), x, g1, bm=512, bk=hd)
        else:
            x = _s5_mixer(x, n1, sc1, sh1, g1, (ssm_w_in, j), ssm_log_step[j],
                          ssm_a_re[j], ssm_a_im[j], ssm_b_re[j], ssm_b_im[j], ssm_c_re[j],
                          ssm_c_im[j], ssm_d[j], (ssm_w_gate, j), (ssm_w_out, j))
        a = _ffn_up(x, norm2_g[i].reshape(1, dm), sc2, sh2, (ffn_w_up, i),
                    ffn_conv_w[i], ffn_conv_b[i].reshape(1, 2 * D_FF))
        x = _mm_res(a.reshape(b * t, D_FF), (ffn_w_down, i), x, g2, bm=512, bk=D_FF, bn=dm // 2)
    return _final_norm(x, final_norm_g.reshape(1, dm))
```
